```python
import jax
import jax.numpy as jnp
from jax import lax
import numpy as np

D_MODEL = 4096
BATCH = 2
SEQ = 8192
DEPTH = 4

GRID_W = 64
CTX_LEN = 256
MIX_W = D_MODEL
GROUP_W = MIX_W // 4

POOL_WINDOWS = (2, 4, 8, 16)
POOL_CH = GROUP_W // len(POOL_WINDOWS)
NA_HEAD_DIM = 64
NA_HEADS = GROUP_W // NA_HEAD_DIM
NA_KH = 8
NA_KW = 16
ROPE_BASE = 10000.0
RW_HEAD = 64
RW_HEADS = GROUP_W // RW_HEAD
RW_LORA = 64
GN_EPS = 64e-5
SGU_CHUNK = 128
SGU_GROUPS = 8
SGU_CH = GROUP_W // SGU_GROUPS
LN_EPS = 1e-5
NORM_EPS = 1e-6

COL_LAYOUT = (
    ('pool_v', GROUP_W), ('pool_z', GROUP_W),
    ('na_q', GROUP_W), ('na_k', GROUP_W), ('na_v', GROUP_W), ('na_z', GROUP_W),
    ('rw_r', GROUP_W), ('rw_k', GROUP_W), ('rw_v', GROUP_W), ('rw_z', GROUP_W),
    ('rw_wd', 2 * RW_LORA), ('rw_ad', 2 * RW_LORA),
    ('sg_u', GROUP_W), ('sg_v', GROUP_W), ('sg_z', GROUP_W),
)
N_IN = sum(size for _, size in COL_LAYOUT)

kernel_name = 'hybrid_pool_natten_rwkv7_sgu_dit'


def rms_norm(x):
    x32 = x.astype(jnp.float32)
    return (x32 * lax.rsqrt(jnp.mean(x32 * x32, axis=-1, keepdims=True) + NORM_EPS)).astype(x.dtype)


def head_rms_norm(x, gain):
    return (rms_norm(x) * gain).astype(x.dtype)


def modulate(cond, w_mod, b_mod):
    m = jax.nn.silu(cond) @ w_mod + b_mod
    return jnp.split(m, 3, axis=-1)


def split_cols(p):
    out, start = {}, 0
    for name, size in COL_LAYOUT:
        out[name] = p[..., start:start + size]
        start += size
    return out


def token_shift(s, reverse):
    if reverse:
        return jnp.concatenate([s[:, 1:], jnp.zeros_like(s[:, :1])], axis=1)
    return jnp.concatenate([jnp.zeros_like(s[:, :1]), s[:, :-1]], axis=1)


def pool_branch(v, pool_w, pool_scale):
    T = v.shape[1]
    v32 = v.astype(jnp.float32)
    csum = jnp.pad(jnp.cumsum(v32, axis=1), ((0, 0), (1, 0), (0, 0)))
    t = jnp.arange(T)
    outs = []
    for g, win in enumerate(POOL_WINDOWS):
        lo = jnp.clip(t - win // 2, 0, T)
        hi = jnp.clip(t - win // 2 + win, 0, T)
        ch = slice(g * POOL_CH, (g + 1) * POOL_CH)
        mean = (csum[:, hi, ch] - csum[:, lo, ch]) / (hi - lo).astype(jnp.float32)[None, :, None]
        outs.append((mean - v32[:, :, ch]).astype(v.dtype) @ pool_w[g])
    return jnp.concatenate(outs, axis=-1) * pool_scale


def rope_2d(x):
    T, dh = x.shape[1], x.shape[-1]
    t = jnp.arange(T)
    n_freq = dh // 4
    inv = ROPE_BASE ** (-jnp.arange(n_freq, dtype=jnp.float32) / n_freq)

    def rot(xa, pos):
        ang = pos.astype(jnp.float32)[:, None] * inv
        cos = jnp.cos(ang)[None, :, None, :]
        sin = jnp.sin(ang)[None, :, None, :]
        x1, x2 = xa[..., :n_freq], xa[..., n_freq:]
        return jnp.concatenate([x1 * cos - x2 * sin, x2 * cos + x1 * sin], axis=-1)

    half = dh // 2
    return jnp.concatenate([rot(x[..., :half], t // GRID_W), rot(x[..., half:], t % GRID_W)],
                           axis=-1).astype(x.dtype)


def na_indices(T):
    rows = T // GRID_W
    kh = min(NA_KH, rows)
    t = jnp.arange(T)
    qr, qc = t // GRID_W, t % GRID_W
    rs = jnp.clip(qr - kh // 2, 0, rows - kh)
    cs = jnp.clip(qc - NA_KW // 2, 0, GRID_W - NA_KW)
    kr = rs[:, None, None] + jnp.arange(kh)[None, :, None]
    kc = cs[:, None, None] + jnp.arange(NA_KW)[None, None, :]
    shape = (T, kh, NA_KW)
    idx = jnp.broadcast_to(kr * GRID_W + kc, shape)
    rel_r = jnp.broadcast_to(kr - qr[:, None, None] + NA_KH - 1, shape)
    rel_c = jnp.broadcast_to(kc - qc[:, None, None] + NA_KW - 1, shape)
    blk = lambda a: a.reshape(rows, GRID_W, kh * NA_KW)
    return blk(idx), blk(rel_r), blk(rel_c)


def na_branch(pl, pc, q_gain, k_gain, rpb, with_ctx):
    hs = lambda a: a.reshape(a.shape[0], a.shape[1], NA_HEADS, NA_HEAD_DIM)
    q = rope_2d(head_rms_norm(hs(pl['na_q']), q_gain))
    k = rope_2d(head_rms_norm(hs(pl['na_k']), k_gain))
    v = hs(pl['na_v'])
    k_c = head_rms_norm(hs(pc['na_k']), k_gain)
    v_c = hs(pc['na_v'])
    B, T = q.shape[0], q.shape[1]
    rows = T // GRID_W
    idx, rel_r, rel_c = na_indices(T)
    scale = NA_HEAD_DIM ** -0.5

    def row_block(args):
        q_b, i_b, rr_b, rc_b = args
        k_w = jnp.take(k, i_b, axis=1)
        v_w = jnp.take(v, i_b, axis=1)
        s_win = jnp.einsum('bqhd,bqkhd->bhqk', q_b, k_w) * scale + rpb[:, rr_b, rc_b]
        s_ctx = jnp.einsum('bqhd,bchd->bhqc', q_b, k_c) * scale
        prob = jax.nn.softmax(jnp.concatenate([s_win, s_ctx], axis=-1).astype(jnp.float32),
                              axis=-1).astype(v.dtype)
        n_win = i_b.shape[-1]
        return (jnp.einsum('bhqk,bqkhd->bqhd', prob[..., :n_win], v_w)
                + jnp.einsum('bhqc,bchd->bqhd', prob[..., n_win:], v_c))

    q_rows = jnp.swapaxes(q.reshape(B, rows, GRID_W, NA_HEADS, NA_HEAD_DIM), 0, 1)
    o = lax.map(row_block, (q_rows, idx, rel_r, rel_c))
    o_lat = jnp.swapaxes(o, 0, 1).reshape(B, T, GROUP_W)
    o_ctx = None
    if with_ctx:
        q_c = head_rms_norm(hs(pc['na_q']), q_gain)
        s = jnp.einsum('bqhd,bkhd->bhqk', q_c, k_c) * scale
        prob = jax.nn.softmax(s.astype(jnp.float32), axis=-1).astype(v_c.dtype)
        o_ctx = jnp.einsum('bhqk,bkhd->bqhd', prob, v_c).reshape(B, -1, GROUP_W)
    return o_lat, o_ctx


def rwkv_dir_inputs(p, d, reverse, mu_rkv, mu_lora, w0, w_up, a0, a_up, k_k, k_a):
    heads = lambda a: a.reshape(a.shape[0], a.shape[1], RW_HEADS, RW_HEAD).astype(jnp.float32)
    lerp = lambda s, m: s + (token_shift(s, reverse) - s) * m
    lo = slice(d * RW_LORA, (d + 1) * RW_LORA)
    r = lerp(p['rw_r'], mu_rkv[d, 0])
    k = lerp(p['rw_k'], mu_rkv[d, 1])
    v = lerp(p['rw_v'], mu_rkv[d, 2])
    wd = lerp(p['rw_wd'][..., lo], mu_lora[d, 0])
    ad = lerp(p['rw_ad'][..., lo], mu_lora[d, 1])
    w = -jax.nn.softplus(-(w0[d] + jnp.tanh(wd) @ w_up[d])) - 0.5
    a = jax.nn.sigmoid(a0[d] + ad @ a_up[d])
    kk = heads(k * k_k)
    kk = kk * lax.rsqrt(jnp.maximum(jnp.sum(kk * kk, axis=-1, keepdims=True), 1e-12))
    k = k * (1 + (a - 1) * k_a)
    decay = jnp.exp(-jnp.exp(heads(w)))
    return (heads(r), decay, heads(k), heads(v), kk, kk * heads(a))


def wkv7_scan(s0, inputs, reverse):
    xs = tuple(jnp.swapaxes(a, 0, 1) for a in inputs)

    def step(S, inp):
        r_t, w_t, k_t, v_t, kk_t, kka_t = inp
        S = (S * w_t[:, :, None, :]
             - jnp.einsum('bhvk,bhk->bhv', S, kk_t)[..., None] * kka_t[:, :, None, :]
             + v_t[..., None] * k_t[:, :, None, :])
        return S, jnp.einsum('bhvk,bhk->bhv', S, r_t)

    S, y = lax.scan(step, s0, xs, reverse=reverse)
    return S, jnp.swapaxes(y, 0, 1)


def rwkv_readout(y, inputs, r_k, ln_w, ln_b):
    r, _, k, v, _, _ = inputs
    B, T = y.shape[0], y.shape[1]
    mu = jnp.mean(y, axis=-1, keepdims=True)
    var = jnp.mean(jnp.square(y - mu), axis=-1, keepdims=True)
    yn = ((y - mu) * lax.rsqrt(var + GN_EPS)).reshape(B, T, GROUP_W) * ln_w + ln_b
    bonus = jnp.sum(r * k * r_k, axis=-1, keepdims=True) * v
    return yn + bonus.reshape(B, T, GROUP_W)


def rwkv_branch(pl, pc, rw, with_ctx):
    mu_rkv, mu_lora, w0, w_up, a0, a_up, k_k, k_a, r_k, ln_w, ln_b = rw
    B = pl['rw_r'].shape[0]
    lat_outs, ctx_outs = [], []
    for d, reverse in enumerate((False, True)):
        ci = rwkv_dir_inputs(pc, d, reverse, mu_rkv, mu_lora, w0, w_up, a0, a_up, k_k, k_a)
        li = rwkv_dir_inputs(pl, d, reverse, mu_rkv, mu_lora, w0, w_up, a0, a_up, k_k, k_a)
        s0 = jnp.zeros((B, RW_HEADS, RW_HEAD, RW_HEAD), jnp.float32)
        s_ctx, y_ctx = wkv7_scan(s0, ci, reverse)
        _, y_lat = wkv7_scan(s_ctx, li, reverse)
        lat_outs.append(rwkv_readout(y_lat, li, r_k, ln_w, ln_b))
        if with_ctx:
            ctx_outs.append(rwkv_readout(y_ctx, ci, r_k, ln_w, ln_b))
    dt = pl['rw_r'].dtype
    o_lat = (lat_outs[0] + lat_outs[1]).astype(dt)
    o_ctx = (ctx_outs[0] + ctx_outs[1]).astype(dt) if with_ctx else None
    return o_lat, o_ctx


def sgu_branch(u, v, ln_w, ln_b, w_s, b_s):
    B, T = v.shape[0], v.shape[1]
    v32 = v.astype(jnp.float32)
    mu = jnp.mean(v32, axis=-1, keepdims=True)
    var = jnp.mean(jnp.square(v32 - mu), axis=-1, keepdims=True)
    vn = (((v32 - mu) * lax.rsqrt(var + LN_EPS)) * ln_w + ln_b).astype(v.dtype)
    vn = vn.reshape(B, T // SGU_CHUNK, SGU_CHUNK, SGU_GROUPS, SGU_CH)
    mixed = jnp.einsum('gpq,bnqgc->bnpgc', w_s, vn) + b_s.T[:, :, None]
    return u * mixed.reshape(B, T, GROUP_W)


def merge(p, y_pool, y_na, y_rw, y_sg, w_out):
    silu = jax.nn.silu
    y = jnp.concatenate([y_pool * silu(p['pool_z']), y_na * silu(p['na_z']),
                         y_rw * silu(p['rw_z']), y_sg * silu(p['sg_z'])], axis=-1)
    return y @ w_out


def hybrid_layer(h_lat, h_ctx, c, c_ctx, w_mod, b_mod, w_in, w_out, pool_w, pool_scale,
                 na_q_gain, na_k_gain, na_rpb, rw, sg_ln_w, sg_ln_b, sg_w, sg_b, with_ctx):
    shift, scale, gate = modulate(c, w_mod, b_mod)
    shift_c, scale_c, gate_c = modulate(c_ctx, w_mod, b_mod)
    x_lat = rms_norm(h_lat) * (1 + scale[:, None]) + shift[:, None]
    x_ctx = rms_norm(h_ctx) * (1 + scale_c) + shift_c
    pl = split_cols(x_lat @ w_in)
    pc = split_cols(x_ctx @ w_in)
    na_lat, na_ctx = na_branch(pl, pc, na_q_gain, na_k_gain, na_rpb, with_ctx)
    rw_lat, rw_ctx = rwkv_branch(pl, pc, rw, with_ctx)
    y_lat = merge(pl, pool_branch(pl['pool_v'], pool_w, pool_scale), na_lat, rw_lat,
                  sgu_branch(pl['sg_u'], pl['sg_v'], sg_ln_w, sg_ln_b, sg_w, sg_b), w_out)
    h_lat = h_lat + gate[:, None] * y_lat
    if with_ctx:
        y_ctx = merge(pc, pool_branch(pc['pool_v'], pool_w, pool_scale), na_ctx, rw_ctx,
                      sgu_branch(pc['sg_u'], pc['sg_v'], sg_ln_w, sg_ln_b, sg_w, sg_b), w_out)
        h_ctx = h_ctx + gate_c * y_ctx
    return h_lat, h_ctx


def setup_inputs(seed: int = 0) -> dict:
    key = jax.random.key(seed)
    ks = iter(jax.random.split(key, 28))
    f32 = jnp.float32
    nrm = lambda shape, s: jax.random.normal(next(ks), shape, f32) * s
    uni = lambda shape, lo, hi: jax.random.uniform(next(ks), shape, f32, lo, hi)
    L, GW = DEPTH, GROUP_W
    return {
        'x': nrm((BATCH, SEQ, D_MODEL), 1.0),
        'c': nrm((BATCH, D_MODEL), 1.0),
        'ctx': nrm((BATCH, CTX_LEN, D_MODEL), 1.0),
        'c_ctx': nrm((D_MODEL,), 1.0),
        'w_mod': nrm((L, D_MODEL, 3 * D_MODEL), 0.5 * D_MODEL ** -0.5),
        'b_mod': nrm((L, 3 * D_MODEL), 0.02),
        'w_in': nrm((L, D_MODEL, N_IN), D_MODEL ** -0.5),
        'w_out': nrm((L, MIX_W, D_MODEL), MIX_W ** -0.5),
        'pool_w': nrm((L, len(POOL_WINDOWS), POOL_CH, POOL_CH), POOL_CH ** -0.5),
        'pool_scale': 1.0 + nrm((L, GW), 0.1),
        'na_q_gain': 1.0 + nrm((L, NA_HEAD_DIM), 0.1),
        'na_k_gain': 1.0 + nrm((L, NA_HEAD_DIM), 0.1),
        'na_rpb': nrm((L, NA_HEADS, 2 * NA_KH - 1, 2 * NA_KW - 1), 0.5),
        'rw_mu_rkv': uni((L, 2, 3, GW), 0.0, 1.0),
        'rw_mu_lora': uni((L, 2, 2, RW_LORA), 0.0, 1.0),
        'rw_w0': uni((L, 2, GW), -5.0, -1.0),
        'rw_w_up': nrm((L, 2, RW_LORA, GW), RW_LORA ** -0.5),
        'rw_a0': nrm((L, 2, GW), 0.5),
        'rw_a_up': nrm((L, 2, RW_LORA, GW), RW_LORA ** -0.5),
        'rw_k_k': 0.85 + nrm((L, GW), 0.05),
        'rw_k_a': 1.0 + nrm((L, GW), 0.05),
        'rw_r_k': nrm((L, RW_HEADS, RW_HEAD), 0.1),
        'rw_ln_w': 1.0 + nrm((L, GW), 0.1),
        'rw_ln_b': nrm((L, GW), 0.02),
        'sg_ln_w': 1.0 + nrm((L, GW), 0.1),
        'sg_ln_b': nrm((L, GW), 0.02),
        'sg_w': nrm((L, SGU_GROUPS, SGU_CHUNK, SGU_CHUNK), SGU_CHUNK ** -0.5),
        'sg_b': nrm((L, SGU_GROUPS, SGU_CHUNK), 0.02),
    }


def reference(x, c, ctx, c_ctx, w_mod, b_mod, w_in, w_out, pool_w, pool_scale,
              na_q_gain, na_k_gain, na_rpb, rw_mu_rkv, rw_mu_lora, rw_w0, rw_w_up, rw_a0,
              rw_a_up, rw_k_k, rw_k_a, rw_r_k, rw_ln_w, rw_ln_b, sg_ln_w, sg_ln_b, sg_w, sg_b):
    h_lat, h_ctx = x, ctx
    for l in range(DEPTH):
        rw = (rw_mu_rkv[l], rw_mu_lora[l], rw_w0[l], rw_w_up[l], rw_a0[l], rw_a_up[l],
              rw_k_k[l], rw_k_a[l], rw_r_k[l], rw_ln_w[l], rw_ln_b[l])
        h_lat, h_ctx = hybrid_layer(
            h_lat, h_ctx, c, c_ctx, w_mod[l], b_mod[l], w_in[l], w_out[l], pool_w[l],
            pool_scale[l], na_q_gain[l], na_k_gain[l], na_rpb[l], rw,
            sg_ln_w[l], sg_ln_b[l], sg_w[l], sg_b[l], with_ctx=(l < DEPTH - 1))
    return h_lat
```

```python
import functools

import jax
import jax.numpy as jnp
from jax import lax
from jax.experimental import pallas as pl
from jax.experimental.pallas import tpu as pltpu

F32 = jnp.float32
BF16 = jnp.bfloat16

LANES = 128
SUBLANES = 8
HEAD = 64
GRID_W = 64
POOL_WINDOWS = (2, 4, 8, 16)
NA_KH = 8
NA_KW = 16
ROPE_BASE = 10000.0
RW_LORA = 64
RW_CHUNK = 64
GN_EPS = 64e-5
SGU_CHUNK = 128
SGU_GROUPS = 8
LN_EPS = 1e-5
NORM_EPS = 1e-6
MASKED = -1e30
VMEM_LIMIT = 56 * 1024 * 1024


def _dot(a, b):
    return jnp.dot(a, b, preferred_element_type=F32)


def _dot_nt(a, b):
    return lax.dot_general(a, b, (((1,), (1,)), ((), ())), preferred_element_type=F32)


def _dot_tn(a, b):
    return lax.dot_general(a, b, (((0,), (0,)), ((), ())), preferred_element_type=F32)


def _silu(x):
    return x * jax.nn.sigmoid(x)


def _split3(x):
    hi = x.astype(BF16)
    r = x - hi.astype(F32)
    mid = r.astype(BF16)
    lo = (r - mid.astype(F32)).astype(BF16)
    return hi, mid, lo


def _iota(shape, dim):
    return lax.broadcasted_iota(jnp.int32, shape, dim)


def _params(*sem):
    return pltpu.CompilerParams(dimension_semantics=sem, vmem_limit_bytes=VMEM_LIMIT)


def _largest_tile(n, unit, cap):
    best = unit
    for t in range(unit, cap + 1, unit):
        if n % t == 0:
            best = t
    return best


def _mod_kernel(c_ref, w_ref, b_ref, o_ref):
    cs = _silu(c_ref[...]).astype(BF16)
    o_ref[0] = _dot(cs, w_ref[0].astype(BF16)) + b_ref[0]


def _modulation(cond, w_mod, b_mod):
    depth, d, n = w_mod.shape
    tn = 512
    return pl.pallas_call(
        _mod_kernel,
        grid=(depth, n // tn),
        in_specs=[
            pl.BlockSpec((SUBLANES, d), lambda l, j: (0, 0)),
            pl.BlockSpec((1, d, tn), lambda l, j: (l, 0, j)),
            pl.BlockSpec((1, 1, tn), lambda l, j: (l, 0, j)),
        ],
        out_specs=pl.BlockSpec((1, SUBLANES, tn), lambda l, j: (l, 0, j)),
        out_shape=jax.ShapeDtypeStruct((depth, SUBLANES, n), F32),
        compiler_params=_params("parallel", "parallel"),
        name="modulation",
    )(cond, w_mod, b_mod.reshape(depth, 1, n))


def _norm_kernel(h_ref, shift_ref, scale_ref, o_ref):
    x = h_ref[...]
    xn = x * lax.rsqrt(jnp.mean(x * x, axis=-1, keepdims=True) + NORM_EPS)
    o_ref[...] = (xn * (1.0 + scale_ref[0]) + shift_ref[0]).astype(BF16)


def _norm_mod(h, mod3, batch, seq_all, ctx_len):
    m, d = h.shape
    tb = _largest_tile(ctx_len, SUBLANES, 256)
    nb = seq_all // tb

    def row(i):
        return jnp.where(i % nb < ctx_len // tb, batch, i // nb)

    return pl.pallas_call(
        _norm_kernel,
        grid=(m // tb,),
        in_specs=[
            pl.BlockSpec((tb, d), lambda i: (i, 0)),
            pl.BlockSpec((1, 1, d), lambda i: (row(i), 0, 0)),
            pl.BlockSpec((1, 1, d), lambda i: (row(i), 0, 1)),
        ],
        out_specs=pl.BlockSpec((tb, d), lambda i: (i, 0)),
        out_shape=jax.ShapeDtypeStruct((m, d), BF16),
        compiler_params=_params("parallel"),
        name="norm_mod",
    )(h, mod3, mod3)


def _mm_kernel(a_ref, w_ref, o_ref):
    o_ref[...] = _dot(a_ref[...], w_ref[...])


def _matmul(a, w, tm, tn):
    m, k = a.shape
    n = w.shape[1]
    return pl.pallas_call(
        _mm_kernel,
        grid=(m // tm, n // tn),
        in_specs=[
            pl.BlockSpec((tm, k), lambda i, j: (i, 0)),
            pl.BlockSpec((k, tn), lambda i, j: (0, j)),
        ],
        out_specs=pl.BlockSpec((tm, tn), lambda i, j: (i, j)),
        out_shape=jax.ShapeDtypeStruct((m, n), F32),
        compiler_params=_params("parallel", "parallel"),
        name="in_proj",
    )(a, w)


def _pool_kernel(tb, nb, ctx_len, seq_all, v_ref, vp_ref, vn_ref, z_ref, pw_ref, ps_ref, o_ref):
    s0 = (pl.program_id(0) % nb) * tb
    is_ctx = s0 < ctx_len
    seq_lo = jnp.where(is_ctx, 0, ctx_len) - s0
    seq_hi = jnp.where(is_ctx, ctx_len, seq_all) - s0
    v = v_ref[...]
    ext = jnp.concatenate([vp_ref[...], v, vn_ref[...]], axis=0)
    halo = SUBLANES
    t = _iota((tb, tb + 2 * halo), 0)
    e = _iota((tb, tb + 2 * halo), 1) - halo
    t1 = _iota((tb, 1), 0)
    pool_ch = v.shape[1] // len(POOL_WINDOWS)
    outs = []
    for g, win in enumerate(POOL_WINDOWS):
        lo = jnp.maximum(t - win // 2, seq_lo)
        hi = jnp.minimum(t - win // 2 + win, seq_hi)
        band = jnp.where((e >= lo) & (e < hi), 1.0, 0.0).astype(BF16)
        cnt = (jnp.minimum(t1 - win // 2 + win, seq_hi) - jnp.maximum(t1 - win // 2, seq_lo)).astype(F32)
        ch = slice(g * pool_ch, (g + 1) * pool_ch)
        p_hi, p_mid, p_lo = _split3(ext[:, ch])
        mean = (_dot(band, p_hi) + _dot(band, p_mid) + _dot(band, p_lo)) / cnt
        outs.append(_dot((mean - v[:, ch]).astype(BF16), pw_ref[g]))
    y = jnp.concatenate(outs, axis=1) * ps_ref[...]
    o_ref[...] = (y * _silu(z_ref[...])).astype(BF16)


def _pool(p_main, pool_w, pool_scale, seq_all, ctx_len, col_v, col_z):
    m = p_main.shape[0]
    gw = pool_scale.shape[-1]
    tb = _largest_tile(ctx_len, SUBLANES, 256)
    nb = seq_all // tb
    hb = tb // SUBLANES
    last = m // SUBLANES - 1
    return pl.pallas_call(
        functools.partial(_pool_kernel, tb, nb, ctx_len, seq_all),
        grid=(m // tb,),
        in_specs=[
            pl.BlockSpec((tb, gw), lambda i: (i, col_v)),
            pl.BlockSpec((SUBLANES, gw), lambda i: (jnp.maximum(i * hb - 1, 0), col_v)),
            pl.BlockSpec((SUBLANES, gw), lambda i: (jnp.minimum((i + 1) * hb, last), col_v)),
            pl.BlockSpec((tb, gw), lambda i: (i, col_z)),
            pl.BlockSpec(pool_w.shape, lambda i: (0, 0, 0)),
            pl.BlockSpec((1, gw), lambda i: (0, 0)),
        ],
        out_specs=pl.BlockSpec((tb, gw), lambda i: (i, 0)),
        out_shape=jax.ShapeDtypeStruct((m, gw), BF16),
        compiler_params=_params("parallel"),
        name="pool",
    )(p_main, p_main, p_main, p_main, pool_w, pool_scale.reshape(1, gw))


def _sgu_kernel(u_ref, v_ref, z_ref, lnw_ref, lnb_ref, ws_ref, bias_ref, o_ref):
    v = v_ref[...]
    d = v - jnp.mean(v, axis=-1, keepdims=True)
    var = jnp.mean(d * d, axis=-1, keepdims=True)
    vn = (d * lax.rsqrt(var + LN_EPS) * lnw_ref[...] + lnb_ref[...]).astype(BF16)
    ch = v.shape[1] // SGU_GROUPS
    mixed = jnp.concatenate(
        [_dot(ws_ref[g], vn[:, g * ch:(g + 1) * ch]) for g in range(SGU_GROUPS)], axis=1)
    o_ref[...] = (u_ref[...] * (mixed + bias_ref[...]) * _silu(z_ref[...])).astype(BF16)


def _sgu(p_main, ln_w, ln_b, w_s, b_s, col_u, col_v, col_z):
    m = p_main.shape[0]
    gw = ln_w.shape[-1]
    bias = jnp.repeat(b_s.T, gw // SGU_GROUPS, axis=1)
    tok = lambda col: pl.BlockSpec((SGU_CHUNK, gw), lambda i: (i, col))
    vec = pl.BlockSpec((1, gw), lambda i: (0, 0))
    return pl.pallas_call(
        _sgu_kernel,
        grid=(m // SGU_CHUNK,),
        in_specs=[tok(col_u), tok(col_v), tok(col_z), vec, vec,
                  pl.BlockSpec(w_s.shape, lambda i: (0, 0, 0)),
                  pl.BlockSpec((SGU_CHUNK, gw), lambda i: (0, 0))],
        out_specs=pl.BlockSpec((SGU_CHUNK, gw), lambda i: (i, 0)),
        out_shape=jax.ShapeDtypeStruct((m, gw), BF16),
        compiler_params=_params("parallel"),
        name="sgu",
    )(p_main, p_main, p_main, ln_w.reshape(1, gw), ln_b.reshape(1, gw), w_s.astype(BF16), bias)


def _na_prep_kernel(q_ref, k_ref, v_ref, cos_ref, sin_ref, qg_ref, kg_ref, ones_ref, qo_ref, ko_ref, vo_ref):
    cos = cos_ref[...]
    sin = sin_ref[...]
    ones = ones_ref[...]
    quarter = HEAD // 4
    low_quarter = (_iota((1, LANES), 1) % (2 * quarter)) < quarter

    def prep(x_ref, gain, scale, o_ref):
        for j in range(x_ref.shape[1] // LANES):
            sl = slice(j * LANES, (j + 1) * LANES)
            x = x_ref[:, sl]
            sq = x * x
            sq_hi = sq.astype(BF16)
            sq_lo = (sq - sq_hi.astype(F32)).astype(BF16)
            ms = (_dot(sq_hi, ones) + _dot(sq_lo, ones)) * (1.0 / HEAD)
            xn = x * lax.rsqrt(ms + NORM_EPS) * gain
            partner = jnp.where(low_quarter, pltpu.roll(xn, LANES - quarter, 1), pltpu.roll(xn, quarter, 1))
            o_ref[:, sl] = ((xn * cos + partner * sin) * scale).astype(BF16)

    prep(q_ref, qg_ref[...], HEAD ** -0.5, qo_ref)
    prep(k_ref, kg_ref[...], 1.0, ko_ref)
    vo_ref[...] = v_ref[...].astype(BF16)


def _rope_tables(seq, ctx_len):
    t = jnp.arange(seq)
    n_freq = HEAD // 4
    inv = ROPE_BASE ** (-jnp.arange(n_freq, dtype=F32) / n_freq)
    ang_r = (t // GRID_W).astype(F32)[:, None] * inv
    ang_c = (t % GRID_W).astype(F32)[:, None] * inv
    cos = jnp.concatenate([jnp.cos(ang_r), jnp.cos(ang_r), jnp.cos(ang_c), jnp.cos(ang_c)], axis=1)
    sin = jnp.concatenate([-jnp.sin(ang_r), jnp.sin(ang_r), -jnp.sin(ang_c), jnp.sin(ang_c)], axis=1)
    cos = jnp.concatenate([jnp.ones((ctx_len, HEAD), F32), cos], axis=0)
    sin = jnp.concatenate([jnp.zeros((ctx_len, HEAD), F32), sin], axis=0)
    reps = LANES // HEAD
    return jnp.tile(cos, (1, reps)), jnp.tile(sin, (1, reps))


def _na_prep(p_main, cos, sin, q_gain, k_gain, seq_all, ctx_len, col_q, col_k, col_v, gw):
    m = p_main.shape[0]
    tb = _largest_tile(ctx_len, SUBLANES, 256)
    nb = seq_all // tb
    reps = LANES // HEAD
    lane_head = jnp.arange(LANES) // HEAD
    ones = (lane_head[:, None] == lane_head[None, :]).astype(BF16)
    tok = lambda col: pl.BlockSpec((tb, gw), lambda i: (i, col))
    tab = pl.BlockSpec((tb, LANES), lambda i: (i % nb, 0))
    vec = pl.BlockSpec((1, LANES), lambda i: (0, 0))
    out = jax.ShapeDtypeStruct((m, gw), BF16)
    return pl.pallas_call(
        _na_prep_kernel,
        grid=(m // tb,),
        in_specs=[tok(col_q), tok(col_k), tok(col_v), tab, tab, vec, vec,
                  pl.BlockSpec((LANES, LANES), lambda i: (0, 0))],
        out_specs=[pl.BlockSpec((tb, gw), lambda i: (i, 0))] * 3,
        out_shape=[out, out, out],
        compiler_params=_params("parallel"),
        name="na_prep",
    )(p_main, p_main, p_main, cos, sin, jnp.tile(q_gain, reps).reshape(1, LANES),
      jnp.tile(k_gain, reps).reshape(1, LANES), ones)


def _na_bias(rpb):
    d = jnp.arange(NA_KH)[:, None]
    i = jnp.arange(NA_KH)[None, :]
    rel_r = i - d + NA_KH - 1
    qc = jnp.arange(GRID_W)[:, None]
    kc = jnp.arange(GRID_W)[None, :]
    cs = jnp.clip(qc - NA_KW // 2, 0, GRID_W - NA_KW)
    valid = (kc >= cs) & (kc < cs + NA_KW)
    rel_c = jnp.clip(kc - qc + NA_KW - 1, 0, 2 * NA_KW - 2)
    t = rpb[:, rel_r][:, :, :, rel_c]
    t = jnp.where(valid, t, MASKED)
    heads = rpb.shape[0]
    reps = LANES // HEAD
    t = t.transpose(1, 0, 3, 2, 4).reshape(NA_KH, heads // reps, reps * GRID_W, NA_KH * GRID_W)
    return jnp.concatenate([t, jnp.full((1,) + t.shape[1:], MASKED, F32)], axis=0)


def _na_kernel(q_ref, kw_ref, vw_ref, kc_ref, vc_ref, bias_ref, z_ref, o_ref):
    head0 = _iota((1, LANES), 1) < HEAD
    rows = q_ref.shape[0]
    for j in range(q_ref.shape[1] // LANES):
        sl = slice(j * LANES, (j + 1) * LANES)
        q = q_ref[:, sl]
        zero = jnp.zeros_like(q)
        qs = jnp.concatenate([jnp.where(head0, q, zero), jnp.where(head0, zero, q)], axis=0)
        s_w = _dot_nt(qs, kw_ref[:, sl]) + bias_ref[0, j]
        s_c = _dot_nt(qs, kc_ref[:, sl])
        mx = jnp.maximum(jnp.max(s_w, axis=-1, keepdims=True), jnp.max(s_c, axis=-1, keepdims=True))
        p_w = jnp.exp(s_w - mx)
        p_c = jnp.exp(s_c - mx)
        den = jnp.sum(p_w, axis=-1, keepdims=True) + jnp.sum(p_c, axis=-1, keepdims=True)
        o = (_dot(p_w.astype(BF16), vw_ref[:, sl]) + _dot(p_c.astype(BF16), vc_ref[:, sl])) / den
        y = jnp.where(head0, o[:rows], o[rows:])
        o_ref[:, sl] = (y * _silu(z_ref[:, sl])).astype(BF16)


def _na(qn, kn, vb, p_main, bias, batch, seq, ctx_len, col_z):
    m, gw = qn.shape
    seq_all = seq + ctx_len
    rows = seq // GRID_W
    assert rows >= NA_KH and ctx_len % GRID_W == 0 and seq_all % ctx_len == 0
    win = NA_KH * GRID_W

    def q_blk(b, r):
        return jnp.where(r < rows, (b * seq_all + ctx_len) // GRID_W + r, (b * seq_all) // GRID_W + r - rows)

    def win_start(b, r):
        start = b * seq_all + ctx_len + jnp.clip(r - NA_KH // 2, 0, rows - NA_KH) * GRID_W
        return pl.multiple_of(start, GRID_W)

    def bias_blk(b, r):
        return jnp.where(r < rows, r - jnp.clip(r - NA_KH // 2, 0, rows - NA_KH), NA_KH)

    window = pl.BlockSpec((pl.Element(win), pl.Element(gw)), lambda b, r: (win_start(b, r), 0))
    ctx_kv = pl.BlockSpec((ctx_len, gw), lambda b, r: (b * (seq_all // ctx_len), 0))
    return pl.pallas_call(
        _na_kernel,
        grid=(batch, rows + ctx_len // GRID_W),
        in_specs=[
            pl.BlockSpec((GRID_W, gw), lambda b, r: (q_blk(b, r), 0)),
            window, window, ctx_kv, ctx_kv,
            pl.BlockSpec((1,) + bias.shape[1:], lambda b, r: (bias_blk(b, r), 0, 0, 0)),
            pl.BlockSpec((GRID_W, gw), lambda b, r: (q_blk(b, r), col_z)),
        ],
        out_specs=pl.BlockSpec((GRID_W, gw), lambda b, r: (q_blk(b, r), 0)),
        out_shape=jax.ShapeDtypeStruct((m, gw), BF16),
        compiler_params=_params("parallel", "arbitrary"),
        name="na_attention",
    )(qn, kn, vb, kn, vb, bias, p_main)


(_MU_R, _MU_K, _MU_V, _W0, _A0, _K_K, _K_A, _R_K, _LN_W, _LN_B) = range(10)


def _rwkv_kernel(reverse, chunk, n_ctx_chunks, r_ref, k_ref, v_ref, lo_ref, rh_ref, kh_ref, vh_ref, loh_ref,
                 vec_ref, mul_ref, wup_ref, aup_ref, *rest):
    if reverse:
        fwd_ref, z_ref, o_ref, state_ref = rest
    else:
        o_ref, state_ref = rest
    step = pl.program_id(1)

    @pl.when(step == 0)
    def _():
        state_ref[...] = jnp.zeros_like(state_ref)

    seq_start = jnp.logical_or(step == 0, step == n_ctx_chunks)
    row_id = _iota((chunk, 1), 0)

    def lerp_shift(x_ref, halo_ref, mu):
        x = x_ref[...]
        if reverse:
            edge, rolled, edge_row = halo_ref[0:1, :], pltpu.roll(x, chunk - 1, 0), chunk - 1
        else:
            edge, rolled, edge_row = halo_ref[SUBLANES - 1:SUBLANES, :], pltpu.roll(x, 1, 0), 0
        edge = jnp.where(seq_start, jnp.zeros_like(edge), edge)
        return x + (jnp.where(row_id == edge_row, edge, rolled) - x) * mu

    vec = lambda n: vec_ref[n:n + 1, :]
    rl = lerp_shift(r_ref, rh_ref, vec(_MU_R))
    kl = lerp_shift(k_ref, kh_ref, vec(_MU_K))
    vl = lerp_shift(v_ref, vh_ref, vec(_MU_V))
    lol = lerp_shift(lo_ref, loh_ref, mul_ref[...])
    neg_w_pre = -(vec(_W0) + _dot(jnp.tanh(lol).astype(BF16), wup_ref[...]))
    w = -(jnp.maximum(neg_w_pre, 0.0) + jnp.log1p(jnp.exp(-jnp.abs(neg_w_pre)))) - 0.5
    ew = jnp.exp(w)
    a = jax.nn.sigmoid(vec(_A0) + _dot(lol.astype(BF16), aup_ref[...]))

    tt = _iota((chunk, chunk), 0)
    ss = _iota((chunk, chunk), 1)
    tri = jnp.where((ss >= tt) if reverse else (ss <= tt), 1.0, 0.0).astype(BF16)
    e_hi, e_mid, e_lo = _split3(ew)
    cs = _dot(tri, e_hi) + _dot(tri, e_mid) + _dot(tri, e_lo)
    tot = cs[0:1, :] if reverse else cs[chunk - 1:chunk, :]
    kk_raw = kl * vec(_K_K)
    kt = kl * (1.0 + (a - 1.0) * vec(_K_A))
    w_incl = jnp.exp(-cs)
    w_excl = jnp.exp(ew - cs)
    w_inv = jnp.exp(cs)
    w_rem = jnp.exp(cs - tot)
    w_tot = jnp.exp(-tot)

    head0 = _iota((1, LANES), 1) < HEAD
    n2 = 2 * chunk
    rr = _iota((n2, n2), 0)
    cc = _iota((n2, n2), 1)
    same_head = (rr ^ cc) < chunk
    t_loc = rr & (chunk - 1)
    s_loc = cc & (chunk - 1)
    earlier = same_head & ((s_loc > t_loc) if reverse else (s_loc < t_loc))
    earlier_eq = same_head & ((s_loc >= t_loc) if reverse else (s_loc <= t_loc))
    levels = chunk.bit_length() - 1
    later_half = 0 if reverse else 1
    sibling = [(((rr >> lb) ^ (cc >> lb)) == 1) & (((rr >> lb) & 1) == later_half) for lb in range(levels)]
    eye = jnp.where(rr == cc, 1.0, 0.0)
    lane_eye = _iota((LANES, LANES), 0) == _iota((LANES, LANES), 1)

    def stack(x):
        z = jnp.zeros_like(x)
        return jnp.concatenate([jnp.where(head0, x, z), jnp.where(head0, z, x)], axis=0)

    def head_sum(x):
        s0 = jnp.sum(jnp.where(head0, x, 0.0), axis=-1, keepdims=True)
        s1 = jnp.sum(jnp.where(head0, 0.0, x), axis=-1, keepdims=True)
        return jnp.where(head0, s0, s1)

    for j in range(r_ref.shape[1] // LANES):
        sl = slice(j * LANES, (j + 1) * LANES)
        kk = kk_raw[:, sl]
        kk = kk * lax.rsqrt(jnp.maximum(head_sum(kk * kk), 1e-12))
        r_p, k_p, v_p = rl[:, sl], kt[:, sl], vl[:, sl]
        beta = kk * a[:, sl]
        kap = stack(kk * w_excl[:, sl]).astype(BF16)
        bh = stack(beta * w_inv[:, sl]).astype(BF16)
        kh = stack(k_p * w_inv[:, sl]).astype(BF16)
        rh32 = stack(r_p * w_incl[:, sl])
        rh = rh32.astype(BF16)
        kw = stack(k_p * w_rem[:, sl]).astype(BF16)
        bw = stack(beta * w_rem[:, sl]).astype(BF16)
        vs = stack(v_p).astype(BF16)

        a_kb = _dot_nt(kap, bh)
        a_kk = jnp.where(earlier, _dot_nt(kap, kh), 0.0).astype(BF16)
        a_rb = jnp.where(earlier_eq, _dot_nt(rh, bh), 0.0).astype(BF16)
        a_rk = jnp.where(earlier_eq, _dot_nt(rh, kh), 0.0).astype(BF16)

        inv = eye - jnp.where(sibling[0], a_kb, 0.0)
        for lb in range(1, levels):
            off = jnp.where(sibling[lb], a_kb, 0.0).astype(BF16)
            inv_b = inv.astype(BF16)
            inv = inv - _dot(_dot(inv_b, off).astype(BF16), inv_b)
        inv_b = inv.astype(BF16)

        p_mat = _dot(inv_b, kap).astype(BF16)
        q_mat = _dot(inv_b, _dot(a_kk, vs).astype(BF16)).astype(BF16)
        r_eff = (rh32 - _dot(a_rb, p_mat)).astype(BF16)
        y0 = _dot(a_rk, vs) - _dot(a_rb, q_mat)
        bp = _dot_tn(bw, p_mat).astype(BF16)
        h_add = _dot_tn(kw, vs) - _dot_tn(bw, q_mat)

        h0 = state_ref[j]
        h0_b = h0.astype(BF16)
        ys = _dot(r_eff, h0_b) + y0
        w_col = jnp.sum(jnp.where(lane_eye, w_tot[:, sl], 0.0), axis=1, keepdims=True)
        state_ref[j] = h0 * w_col - _dot(bp, h0_b) + h_add

        y = ys[:chunk] + ys[chunk:]
        d = y - head_sum(y) * (1.0 / HEAD)
        var = head_sum(d * d) * (1.0 / HEAD)
        out = d * lax.rsqrt(var + GN_EPS) * vec(_LN_W)[:, sl] + vec(_LN_B)[:, sl]
        out = out + head_sum(r_p * k_p * vec(_R_K)[:, sl]) * v_p
        if reverse:
            o_ref[:, sl] = ((fwd_ref[:, sl] + out) * _silu(z_ref[:, sl])).astype(BF16)
        else:
            o_ref[:, sl] = out


def _rwkv_dir(reverse, p_main, p_lora, vecs, mu_lora, w_up, a_up, fwd_out, batch, seq_all, ctx_len,
              col_r, col_k, col_v, col_z, gw):
    m = p_main.shape[0]
    chunk = RW_CHUNK
    assert ctx_len % chunk == 0 and seq_all % chunk == 0
    n_chunks = seq_all // chunk
    n_ctx = ctx_len // chunk
    per8 = chunk // SUBLANES
    last8 = m // SUBLANES - 1
    nl = p_lora.shape[1]

    def chunk_of(i):
        if reverse:
            return jnp.where(i < n_ctx, n_ctx - 1 - i, n_chunks - 1 + n_ctx - i)
        return i

    def blk(b, i):
        return b * n_chunks + chunk_of(i)

    def halo(b, i):
        if reverse:
            return jnp.minimum((blk(b, i) + 1) * per8, last8)
        return jnp.maximum(blk(b, i) * per8 - 1, 0)

    tok = lambda col: pl.BlockSpec((chunk, gw), lambda b, i: (blk(b, i), col))
    hal = lambda col: pl.BlockSpec((SUBLANES, gw), lambda b, i: (halo(b, i), col))
    const = lambda arr: pl.BlockSpec(arr.shape, lambda b, i: (0,) * arr.ndim)
    in_specs = [tok(col_r), tok(col_k), tok(col_v), pl.BlockSpec((chunk, nl), lambda b, i: (blk(b, i), 0)),
                hal(col_r), hal(col_k), hal(col_v), pl.BlockSpec((SUBLANES, nl), lambda b, i: (halo(b, i), 0)),
                const(vecs), const(mu_lora), const(w_up), const(a_up)]
    args = [p_main, p_main, p_main, p_lora, p_main, p_main, p_main, p_lora, vecs, mu_lora, w_up, a_up]
    if reverse:
        in_specs += [pl.BlockSpec((chunk, gw), lambda b, i: (blk(b, i), 0)), tok(col_z)]
        args += [fwd_out, p_main]
    return pl.pallas_call(
        functools.partial(_rwkv_kernel, reverse, chunk, n_ctx),
        grid=(batch, n_chunks),
        in_specs=in_specs,
        out_specs=pl.BlockSpec((chunk, gw), lambda b, i: (blk(b, i), 0)),
        out_shape=jax.ShapeDtypeStruct((m, gw), BF16 if reverse else F32),
        scratch_shapes=[pltpu.VMEM((gw // LANES, LANES, LANES), F32)],
        compiler_params=_params("parallel", "arbitrary"),
        name="rwkv_rev" if reverse else "rwkv_fwd",
    )(*args)


def _rwkv(p_main, p_lora, rw, batch, seq_all, ctx_len, cols, gw):
    mu_rkv, mu_lora, w0, w_up, a0, a_up, k_k, k_a, r_k, ln_w, ln_b = rw
    out = None
    for d, reverse in enumerate((False, True)):
        rows = [mu_rkv[d, 0], mu_rkv[d, 1], mu_rkv[d, 2], w0[d], a0[d], k_k, k_a, r_k.reshape(-1), ln_w, ln_b]
        vecs = jnp.zeros((2 * SUBLANES, gw), F32).at[:len(rows)].set(jnp.stack(rows))
        zeros = jnp.zeros((RW_LORA, gw), F32)
        parts_w = [zeros] * 4
        parts_a = [zeros] * 4
        parts_w[d] = w_up[d]
        parts_a[2 + d] = a_up[d]
        mu = jnp.zeros((4, RW_LORA), F32).at[d].set(mu_lora[d, 0]).at[2 + d].set(mu_lora[d, 1])
        out = _rwkv_dir(reverse, p_main, p_lora, vecs, mu.reshape(1, 4 * RW_LORA),
                        jnp.concatenate(parts_w).astype(BF16), jnp.concatenate(parts_a).astype(BF16),
                        out, batch, seq_all, ctx_len, *cols, gw)
    return out


def _merge_kernel(tm, nb, ctx_len, y0_ref, y1_ref, y2_ref, y3_ref, w_ref, h_ref, gb_ref, gc_ref, o_ref):
    gw = y0_ref.shape[1]
    acc = _dot(y0_ref[...], w_ref[0:gw, :])
    for g, y_ref in enumerate((y1_ref, y2_ref, y3_ref), start=1):
        acc += _dot(y_ref[...], w_ref[g * gw:(g + 1) * gw, :])
    pos = (pl.program_id(0) % nb) * tm + _iota((tm, 1), 0)
    gate = jnp.where(pos < ctx_len, gc_ref[0], gb_ref[0])
    o_ref[...] = h_ref[...] + gate * acc


def _merge(ys, w_out, h, mod3, batch, seq_all, ctx_len, tm, tn):
    m, d = h.shape
    gw = ys[0].shape[1]
    nb = seq_all // tm
    y_spec = pl.BlockSpec((tm, gw), lambda i, j: (i, 0))
    gate_col = 2 * (d // tn)
    return pl.pallas_call(
        functools.partial(_merge_kernel, tm, nb, ctx_len),
        grid=(m // tm, d // tn),
        in_specs=[y_spec, y_spec, y_spec, y_spec,
                  pl.BlockSpec((w_out.shape[0], tn), lambda i, j: (0, j)),
                  pl.BlockSpec((tm, tn), lambda i, j: (i, j)),
                  pl.BlockSpec((1, 1, tn), lambda i, j: (i // nb, 0, gate_col + j)),
                  pl.BlockSpec((1, 1, tn), lambda i, j: (batch, 0, gate_col + j))],
        out_specs=pl.BlockSpec((tm, tn), lambda i, j: (i, j)),
        out_shape=jax.ShapeDtypeStruct((m, d), F32),
        compiler_params=_params("parallel", "parallel"),
        name="merge",
    )(*ys, w_out, h, mod3, mod3)


def kernel(x, c, ctx, c_ctx, w_mod, b_mod, w_in, w_out, pool_w, pool_scale, na_q_gain, na_k_gain, na_rpb, rw_mu_rkv, rw_mu_lora, rw_w0, rw_w_up, rw_a0, rw_a_up, rw_k_k, rw_k_a, rw_r_k, rw_ln_w, rw_ln_b, sg_ln_w, sg_ln_b, sg_w, sg_b):
    batch, seq, d = x.shape
    ctx_len = ctx.shape[1]
    seq_all = ctx_len + seq
    m = batch * seq_all
    depth = w_mod.shape[0]
    gw = d // 4
    assert batch < SUBLANES and gw % LANES == 0

    h = jnp.concatenate([ctx, x], axis=1).reshape(m, d)
    cond = jnp.zeros((SUBLANES, d), F32).at[:batch].set(c).at[batch].set(c_ctx)
    mod = _modulation(cond, w_mod, b_mod)
    cos, sin = _rope_tables(seq, ctx_len)
    tm = _largest_tile(seq_all, 2 * LANES, 1024)

    lora_lo = 10 * gw
    lora_hi = lora_lo + 4 * RW_LORA
    (c_pool_v, c_pool_z, c_na_q, c_na_k, c_na_v, c_na_z, c_rw_r, c_rw_k, c_rw_v, c_rw_z,
     c_sg_u, c_sg_v, c_sg_z) = range(13)

    for l in range(depth):
        w_main = jnp.concatenate([w_in[l, :, :lora_lo], w_in[l, :, lora_hi:]], axis=1).astype(BF16)
        w_lora = w_in[l, :, lora_lo:lora_hi].astype(BF16)
        mod3 = mod[l].reshape(SUBLANES, 1, 3 * d)
        xn = _norm_mod(h, mod3, batch, seq_all, ctx_len)
        p_main = _matmul(xn, w_main, tm, gw)
        p_lora = _matmul(xn, w_lora, tm, 4 * RW_LORA)

        y_pool = _pool(p_main, pool_w[l].astype(BF16), pool_scale[l], seq_all, ctx_len, c_pool_v, c_pool_z)
        qn, kn, vb = _na_prep(p_main, cos, sin, na_q_gain[l], na_k_gain[l], seq_all, ctx_len,
                              c_na_q, c_na_k, c_na_v, gw)
        y_na = _na(qn, kn, vb, p_main, _na_bias(na_rpb[l]), batch, seq, ctx_len, c_na_z)
        rw = (rw_mu_rkv[l], rw_mu_lora[l], rw_w0[l], rw_w_up[l], rw_a0[l], rw_a_up[l],
              rw_k_k[l], rw_k_a[l], rw_r_k[l], rw_ln_w[l], rw_ln_b[l])
        y_rw = _rwkv(p_main, p_lora, rw, batch, seq_all, ctx_len, (c_rw_r, c_rw_k, c_rw_v, c_rw_z), gw)
        y_sg = _sgu(p_main, sg_ln_w[l], sg_ln_b[l], sg_w[l], sg_b[l], c_sg_u, c_sg_v, c_sg_z)
        h = _merge((y_pool, y_na, y_rw, y_sg), w_out[l].astype(BF16), h, mod3, batch, seq_all, ctx_len, tm, gw)

    return h.reshape(batch, seq_all, d)[:, ctx_len:, :]
```

```python
import functools

import jax
import jax.numpy as jnp
from jax import lax
from jax.experimental import pallas as pl
from jax.experimental.pallas import tpu as pltpu

F32 = jnp.float32
BF16 = jnp.bfloat16

LANES = 128
SUBLANES = 8
HEAD = 64
GRID_W = 64
POOL_WINDOWS = (2, 4, 8, 16)
NA_KH = 8
NA_KW = 16
ROPE_BASE = 10000.0
RW_LORA = 64
RW_CHUNK = 64
GN_EPS = 64e-5
SGU_CHUNK = 128
SGU_GROUPS = 8
LN_EPS = 1e-5
NORM_EPS = 1e-6
MASKED = -1e30
VMEM_LIMIT = 56 * 1024 * 1024


def _dot(a, b):
    return jnp.dot(a, b, preferred_element_type=F32)


def _dot_nt(a, b):
    return lax.dot_general(a, b, (((1,), (1,)), ((), ())), preferred_element_type=F32)


def _dot_tn(a, b):
    return lax.dot_general(a, b, (((0,), (0,)), ((), ())), preferred_element_type=F32)


def _silu(x):
    return x * jax.nn.sigmoid(x)


def _split3(x):
    hi = x.astype(BF16)
    r = x - hi.astype(F32)
    mid = r.astype(BF16)
    lo = (r - mid.astype(F32)).astype(BF16)
    return hi, mid, lo


def _iota(shape, dim):
    return lax.broadcasted_iota(jnp.int32, shape, dim)


def _params(*sem):
    return pltpu.CompilerParams(dimension_semantics=sem, vmem_limit_bytes=VMEM_LIMIT)


def _largest_tile(n, unit, cap):
    best = unit
    for t in range(unit, cap + 1, unit):
        if n % t == 0:
            best = t
    return best


def _mod_kernel(c_ref, w_ref, b_ref, o_ref):
    cs = _silu(c_ref[...]).astype(BF16)
    o_ref[0] = _dot(cs, w_ref[0].astype(BF16)) + b_ref[0]


def _modulation(cond, w_mod, b_mod):
    depth, d, n = w_mod.shape
    tn = 512
    return pl.pallas_call(
        _mod_kernel,
        grid=(depth, n // tn),
        in_specs=[
            pl.BlockSpec((SUBLANES, d), lambda l, j: (0, 0)),
            pl.BlockSpec((1, d, tn), lambda l, j: (l, 0, j)),
            pl.BlockSpec((1, 1, tn), lambda l, j: (l, 0, j)),
        ],
        out_specs=pl.BlockSpec((1, SUBLANES, tn), lambda l, j: (l, 0, j)),
        out_shape=jax.ShapeDtypeStruct((depth, SUBLANES, n), F32),
        compiler_params=_params("parallel", "parallel"),
        name="modulation",
    )(cond, w_mod, b_mod.reshape(depth, 1, n))


def _norm_kernel(h_ref, shift_ref, scale_ref, o_ref):
    x = h_ref[...]
    xn = x * lax.rsqrt(jnp.mean(x * x, axis=-1, keepdims=True) + NORM_EPS)
    o_ref[...] = (xn * (1.0 + scale_ref[0]) + shift_ref[0]).astype(BF16)


def _norm_mod(h, mod3, batch, seq_all, ctx_len):
    m, d = h.shape
    tb = _largest_tile(ctx_len, SUBLANES, 256)
    nb = seq_all // tb

    def row(i):
        return jnp.where(i % nb < ctx_len // tb, batch, i // nb)

    return pl.pallas_call(
        _norm_kernel,
        grid=(m // tb,),
        in_specs=[
            pl.BlockSpec((tb, d), lambda i: (i, 0)),
            pl.BlockSpec((1, 1, d), lambda i: (row(i), 0, 0)),
            pl.BlockSpec((1, 1, d), lambda i: (row(i), 0, 1)),
        ],
        out_specs=pl.BlockSpec((tb, d), lambda i: (i, 0)),
        out_shape=jax.ShapeDtypeStruct((m, d), BF16),
        compiler_params=_params("parallel"),
        name="norm_mod",
    )(h, mod3, mod3)


def _mm_kernel(a_ref, w_ref, o_ref):
    o_ref[...] = _dot(a_ref[...], w_ref[...])


def _matmul(a, w, tm, tn):
    m, k = a.shape
    n = w.shape[1]
    return pl.pallas_call(
        _mm_kernel,
        grid=(m // tm, n // tn),
        in_specs=[
            pl.BlockSpec((tm, k), lambda i, j: (i, 0)),
            pl.BlockSpec((k, tn), lambda i, j: (0, j)),
        ],
        out_specs=pl.BlockSpec((tm, tn), lambda i, j: (i, j)),
        out_shape=jax.ShapeDtypeStruct((m, n), F32),
        compiler_params=_params("parallel", "parallel"),
        name="in_proj",
    )(a, w)


def _pool_kernel(tb, nb, ctx_len, seq_all, v_ref, vp_ref, vn_ref, z_ref, pw_ref, ps_ref, o_ref):
    s0 = (pl.program_id(0) % nb) * tb
    is_ctx = s0 < ctx_len
    seq_lo = jnp.where(is_ctx, 0, ctx_len) - s0
    seq_hi = jnp.where(is_ctx, ctx_len, seq_all) - s0
    v = v_ref[...]
    ext = jnp.concatenate([vp_ref[...], v, vn_ref[...]], axis=0)
    halo = SUBLANES
    t = _iota((tb, tb + 2 * halo), 0)
    e = _iota((tb, tb + 2 * halo), 1) - halo
    t1 = _iota((tb, 1), 0)
    pool_ch = v.shape[1] // len(POOL_WINDOWS)
    outs = []
    for g, win in enumerate(POOL_WINDOWS):
        lo = jnp.maximum(t - win // 2, seq_lo)
        hi = jnp.minimum(t - win // 2 + win, seq_hi)
        band = jnp.where((e >= lo) & (e < hi), 1.0, 0.0).astype(BF16)
        cnt = (jnp.minimum(t1 - win // 2 + win, seq_hi) - jnp.maximum(t1 - win // 2, seq_lo)).astype(F32)
        ch = slice(g * pool_ch, (g + 1) * pool_ch)
        p_hi, p_mid, p_lo = _split3(ext[:, ch])
        mean = (_dot(band, p_hi) + _dot(band, p_mid) + _dot(band, p_lo)) / cnt
        outs.append(_dot((mean - v[:, ch]).astype(BF16), pw_ref[g]))
    y = jnp.concatenate(outs, axis=1) * ps_ref[...]
    o_ref[...] = (y * _silu(z_ref[...])).astype(BF16)


def _pool(p_main, pool_w, pool_scale, seq_all, ctx_len, col_v, col_z):
    m = p_main.shape[0]
    gw = pool_scale.shape[-1]
    tb = _largest_tile(ctx_len, SUBLANES, 256)
    nb = seq_all // tb
    hb = tb // SUBLANES
    last = m // SUBLANES - 1
    return pl.pallas_call(
        functools.partial(_pool_kernel, tb, nb, ctx_len, seq_all),
        grid=(m // tb,),
        in_specs=[
            pl.BlockSpec((tb, gw), lambda i: (i, col_v)),
            pl.BlockSpec((SUBLANES, gw), lambda i: (jnp.maximum(i * hb - 1, 0), col_v)),
            pl.BlockSpec((SUBLANES, gw), lambda i: (jnp.minimum((i + 1) * hb, last), col_v)),
            pl.BlockSpec((tb, gw), lambda i: (i, col_z)),
            pl.BlockSpec(pool_w.shape, lambda i: (0, 0, 0)),
            pl.BlockSpec((1, gw), lambda i: (0, 0)),
        ],
        out_specs=pl.BlockSpec((tb, gw), lambda i: (i, 0)),
        out_shape=jax.ShapeDtypeStruct((m, gw), BF16),
        compiler_params=_params("parallel"),
        name="pool",
    )(p_main, p_main, p_main, p_main, pool_w, pool_scale.reshape(1, gw))


def _sgu_kernel(u_ref, v_ref, z_ref, lnw_ref, lnb_ref, ws_ref, bias_ref, o_ref):
    v = v_ref[...]
    d = v - jnp.mean(v, axis=-1, keepdims=True)
    var = jnp.mean(d * d, axis=-1, keepdims=True)
    vn = (d * lax.rsqrt(var + LN_EPS) * lnw_ref[...] + lnb_ref[...]).astype(BF16)
    ch = v.shape[1] // SGU_GROUPS
    mixed = jnp.concatenate(
        [_dot(ws_ref[g], vn[:, g * ch:(g + 1) * ch]) for g in range(SGU_GROUPS)], axis=1)
    o_ref[...] = (u_ref[...] * (mixed + bias_ref[...]) * _silu(z_ref[...])).astype(BF16)


def _sgu(p_main, ln_w, ln_b, w_s, b_s, col_u, col_v, col_z):
    m = p_main.shape[0]
    gw = ln_w.shape[-1]
    bias = jnp.repeat(b_s.T, gw // SGU_GROUPS, axis=1)
    tok = lambda col: pl.BlockSpec((SGU_CHUNK, gw), lambda i: (i, col))
    vec = pl.BlockSpec((1, gw), lambda i: (0, 0))
    return pl.pallas_call(
        _sgu_kernel,
        grid=(m // SGU_CHUNK,),
        in_specs=[tok(col_u), tok(col_v), tok(col_z), vec, vec,
                  pl.BlockSpec(w_s.shape, lambda i: (0, 0, 0)),
                  pl.BlockSpec((SGU_CHUNK, gw), lambda i: (0, 0))],
        out_specs=pl.BlockSpec((SGU_CHUNK, gw), lambda i: (i, 0)),
        out_shape=jax.ShapeDtypeStruct((m, gw), BF16),
        compiler_params=_params("parallel"),
        name="sgu",
    )(p_main, p_main, p_main, ln_w.reshape(1, gw), ln_b.reshape(1, gw), w_s.astype(BF16), bias)


def _na_prep_kernel(q_ref, k_ref, v_ref, cos_ref, sin_ref, qg_ref, kg_ref, ones_ref, qo_ref, ko_ref, vo_ref):
    cos = cos_ref[...]
    sin = sin_ref[...]
    ones = ones_ref[...]
    quarter = HEAD // 4
    low_quarter = (_iota((1, LANES), 1) % (2 * quarter)) < quarter

    def prep(x_ref, gain, scale, o_ref):
        for j in range(x_ref.shape[1] // LANES):
            sl = slice(j * LANES, (j + 1) * LANES)
            x = x_ref[:, sl]
            sq = x * x
            sq_hi = sq.astype(BF16)
            sq_lo = (sq - sq_hi.astype(F32)).astype(BF16)
            ms = (_dot(sq_hi, ones) + _dot(sq_lo, ones)) * (1.0 / HEAD)
            xn = x * lax.rsqrt(ms + NORM_EPS) * gain
            partner = jnp.where(low_quarter, pltpu.roll(xn, LANES - quarter, 1), pltpu.roll(xn, quarter, 1))
            o_ref[:, sl] = ((xn * cos + partner * sin) * scale).astype(BF16)

    prep(q_ref, qg_ref[...], HEAD ** -0.5, qo_ref)
    prep(k_ref, kg_ref[...], 1.0, ko_ref)
    vo_ref[...] = v_ref[...].astype(BF16)


def _rope_tables(seq, ctx_len):
    t = jnp.arange(seq)
    n_freq = HEAD // 4
    inv = ROPE_BASE ** (-jnp.arange(n_freq, dtype=F32) / n_freq)
    ang_r = (t // GRID_W).astype(F32)[:, None] * inv
    ang_c = (t % GRID_W).astype(F32)[:, None] * inv
    cos = jnp.concatenate([jnp.cos(ang_r), jnp.cos(ang_r), jnp.cos(ang_c), jnp.cos(ang_c)], axis=1)
    sin = jnp.concatenate([-jnp.sin(ang_r), jnp.sin(ang_r), -jnp.sin(ang_c), jnp.sin(ang_c)], axis=1)
    cos = jnp.concatenate([jnp.ones((ctx_len, HEAD), F32), cos], axis=0)
    sin = jnp.concatenate([jnp.zeros((ctx_len, HEAD), F32), sin], axis=0)
    reps = LANES // HEAD
    return jnp.tile(cos, (1, reps)), jnp.tile(sin, (1, reps))


def _na_prep(p_main, cos, sin, q_gain, k_gain, seq_all, ctx_len, col_q, col_k, col_v, gw):
    m = p_main.shape[0]
    tb = _largest_tile(ctx_len, SUBLANES, 256)
    nb = seq_all // tb
    reps = LANES // HEAD
    lane_head = jnp.arange(LANES) // HEAD
    ones = (lane_head[:, None] == lane_head[None, :]).astype(BF16)
    tok = lambda col: pl.BlockSpec((tb, gw), lambda i: (i, col))
    tab = pl.BlockSpec((tb, LANES), lambda i: (i % nb, 0))
    vec = pl.BlockSpec((1, LANES), lambda i: (0, 0))
    out = jax.ShapeDtypeStruct((m, gw), BF16)
    return pl.pallas_call(
        _na_prep_kernel,
        grid=(m // tb,),
        in_specs=[tok(col_q), tok(col_k), tok(col_v), tab, tab, vec, vec,
                  pl.BlockSpec((LANES, LANES), lambda i: (0, 0))],
        out_specs=[pl.BlockSpec((tb, gw), lambda i: (i, 0))] * 3,
        out_shape=[out, out, out],
        compiler_params=_params("parallel"),
        name="na_prep",
    )(p_main, p_main, p_main, cos, sin, jnp.tile(q_gain, reps).reshape(1, LANES),
      jnp.tile(k_gain, reps).reshape(1, LANES), ones)


def _na_bias(rpb):
    d = jnp.arange(NA_KH)[:, None]
    i = jnp.arange(NA_KH)[None, :]
    rel_r = i - d + NA_KH - 1
    qc = jnp.arange(GRID_W)[:, None]
    kc = jnp.arange(GRID_W)[None, :]
    cs = jnp.clip(qc - NA_KW // 2, 0, GRID_W - NA_KW)
    valid = (kc >= cs) & (kc < cs + NA_KW)
    rel_c = jnp.clip(kc - qc + NA_KW - 1, 0, 2 * NA_KW - 2)
    t = rpb[:, rel_r][:, :, :, rel_c]
    t = jnp.where(valid, t, MASKED)
    heads = rpb.shape[0]
    reps = LANES // HEAD
    t = t.transpose(1, 0, 3, 2, 4).reshape(NA_KH, heads // reps, reps * GRID_W, NA_KH * GRID_W)
    return jnp.concatenate([t, jnp.full((1,) + t.shape[1:], MASKED, F32)], axis=0)


def _na_kernel(q_ref, kw_ref, vw_ref, kc_ref, vc_ref, bias_ref, z_ref, o_ref):
    head0 = _iota((1, LANES), 1) < HEAD
    rows = q_ref.shape[0]
    tiles = [slice(j * LANES, (j + 1) * LANES) for j in range(q_ref.shape[1] // LANES)]
    n = range(len(tiles))
    qs = []
    for sl in tiles:
        q = q_ref[:, sl]
        zero = jnp.zeros_like(q)
        qs.append(jnp.concatenate([jnp.where(head0, q, zero), jnp.where(head0, zero, q)], axis=0))
    s_w = [_dot_nt(qs[j], kw_ref[:, tiles[j]]) + bias_ref[0, j] for j in n]
    s_c = [_dot_nt(qs[j], kc_ref[:, tiles[j]]) for j in n]
    mx = [jnp.maximum(jnp.max(s_w[j], axis=-1, keepdims=True), jnp.max(s_c[j], axis=-1, keepdims=True)) for j in n]
    p_w = [jnp.exp(s_w[j] - mx[j]) for j in n]
    p_c = [jnp.exp(s_c[j] - mx[j]) for j in n]
    den = [jnp.sum(p_w[j], axis=-1, keepdims=True) + jnp.sum(p_c[j], axis=-1, keepdims=True) for j in n]
    o = [_dot(p_w[j].astype(BF16), vw_ref[:, tiles[j]]) + _dot(p_c[j].astype(BF16), vc_ref[:, tiles[j]]) for j in n]
    for j, sl in enumerate(tiles):
        oj = o[j] / den[j]
        y = jnp.where(head0, oj[:rows], oj[rows:])
        o_ref[:, sl] = (y * _silu(z_ref[:, sl])).astype(BF16)


def _na(qn, kn, vb, p_main, bias, batch, seq, ctx_len, col_z):
    m, gw = qn.shape
    seq_all = seq + ctx_len
    rows = seq // GRID_W
    assert rows >= NA_KH and ctx_len % GRID_W == 0 and seq_all % ctx_len == 0
    win = NA_KH * GRID_W

    def q_blk(b, r):
        return jnp.where(r < rows, (b * seq_all + ctx_len) // GRID_W + r, (b * seq_all) // GRID_W + r - rows)

    def win_start(b, r):
        start = b * seq_all + ctx_len + jnp.clip(r - NA_KH // 2, 0, rows - NA_KH) * GRID_W
        return pl.multiple_of(start, GRID_W)

    def bias_blk(b, r):
        return jnp.where(r < rows, r - jnp.clip(r - NA_KH // 2, 0, rows - NA_KH), NA_KH)

    window = pl.BlockSpec((pl.Element(win), pl.Element(gw)), lambda b, r: (win_start(b, r), 0))
    ctx_kv = pl.BlockSpec((ctx_len, gw), lambda b, r: (b * (seq_all // ctx_len), 0))
    return pl.pallas_call(
        _na_kernel,
        grid=(batch, rows + ctx_len // GRID_W),
        in_specs=[
            pl.BlockSpec((GRID_W, gw), lambda b, r: (q_blk(b, r), 0)),
            window, window, ctx_kv, ctx_kv,
            pl.BlockSpec((1,) + bias.shape[1:], lambda b, r: (bias_blk(b, r), 0, 0, 0)),
            pl.BlockSpec((GRID_W, gw), lambda b, r: (q_blk(b, r), col_z)),
        ],
        out_specs=pl.BlockSpec((GRID_W, gw), lambda b, r: (q_blk(b, r), 0)),
        out_shape=jax.ShapeDtypeStruct((m, gw), BF16),
        compiler_params=_params("parallel", "arbitrary"),
        name="na_attention",
    )(qn, kn, vb, kn, vb, bias, p_main)


(_MU_R, _MU_K, _MU_V, _W0, _A0, _K_K, _K_A, _R_K, _LN_W, _LN_B) = range(10)


def _rwkv_kernel(reverse, chunk, n_ctx_chunks, r_ref, k_ref, v_ref, lo_ref, rh_ref, kh_ref, vh_ref, loh_ref,
                 vec_ref, mul_ref, wup_ref, aup_ref, *rest):
    if reverse:
        fwd_ref, z_ref, o_ref, state_ref = rest
    else:
        o_ref, state_ref = rest
    step = pl.program_id(1)

    @pl.when(step == 0)
    def _():
        state_ref[...] = jnp.zeros_like(state_ref)

    seq_start = jnp.logical_or(step == 0, step == n_ctx_chunks)
    row_id = _iota((chunk, 1), 0)

    def lerp_shift(x_ref, halo_ref, mu):
        x = x_ref[...]
        if reverse:
            edge, rolled, edge_row = halo_ref[0:1, :], pltpu.roll(x, chunk - 1, 0), chunk - 1
        else:
            edge, rolled, edge_row = halo_ref[SUBLANES - 1:SUBLANES, :], pltpu.roll(x, 1, 0), 0
        edge = jnp.where(seq_start, jnp.zeros_like(edge), edge)
        return x + (jnp.where(row_id == edge_row, edge, rolled) - x) * mu

    vec = lambda n: vec_ref[n:n + 1, :]
    rl = lerp_shift(r_ref, rh_ref, vec(_MU_R))
    kl = lerp_shift(k_ref, kh_ref, vec(_MU_K))
    vl = lerp_shift(v_ref, vh_ref, vec(_MU_V))
    lol = lerp_shift(lo_ref, loh_ref, mul_ref[...])
    neg_w_pre = -(vec(_W0) + _dot(jnp.tanh(lol).astype(BF16), wup_ref[...]))
    w = -(jnp.maximum(neg_w_pre, 0.0) + jnp.log1p(jnp.exp(-jnp.abs(neg_w_pre)))) - 0.5
    ew = jnp.exp(w)
    a = jax.nn.sigmoid(vec(_A0) + _dot(lol.astype(BF16), aup_ref[...]))

    tt = _iota((chunk, chunk), 0)
    ss = _iota((chunk, chunk), 1)
    tri = jnp.where((ss >= tt) if reverse else (ss <= tt), 1.0, 0.0).astype(BF16)
    e_hi, e_mid, e_lo = _split3(ew)
    cs = _dot(tri, e_hi) + _dot(tri, e_mid) + _dot(tri, e_lo)
    tot = cs[0:1, :] if reverse else cs[chunk - 1:chunk, :]
    kk_raw = kl * vec(_K_K)
    kt = kl * (1.0 + (a - 1.0) * vec(_K_A))
    w_incl = jnp.exp(-cs)
    w_excl = jnp.exp(ew - cs)
    w_inv = jnp.exp(cs)
    w_rem = jnp.exp(cs - tot)
    w_tot = jnp.exp(-tot)

    head0 = _iota((1, LANES), 1) < HEAD
    n2 = 2 * chunk
    rr = _iota((n2, n2), 0)
    cc = _iota((n2, n2), 1)
    same_head = (rr ^ cc) < chunk
    t_loc = rr & (chunk - 1)
    s_loc = cc & (chunk - 1)
    earlier = same_head & ((s_loc > t_loc) if reverse else (s_loc < t_loc))
    earlier_eq = same_head & ((s_loc >= t_loc) if reverse else (s_loc <= t_loc))
    levels = chunk.bit_length() - 1
    later_half = 0 if reverse else 1
    sibling = [(((rr >> lb) ^ (cc >> lb)) == 1) & (((rr >> lb) & 1) == later_half) for lb in range(levels)]
    eye = jnp.where(rr == cc, 1.0, 0.0)
    lane_eye = _iota((LANES, LANES), 0) == _iota((LANES, LANES), 1)

    def stack(x):
        z = jnp.zeros_like(x)
        return jnp.concatenate([jnp.where(head0, x, z), jnp.where(head0, z, x)], axis=0)

    def head_sum(x):
        s0 = jnp.sum(jnp.where(head0, x, 0.0), axis=-1, keepdims=True)
        s1 = jnp.sum(jnp.where(head0, 0.0, x), axis=-1, keepdims=True)
        return jnp.where(head0, s0, s1)

    tiles = [slice(j * LANES, (j + 1) * LANES) for j in range(r_ref.shape[1] // LANES)]
    kap, bh, kh, rh32, rh, kw, bw, vs = [], [], [], [], [], [], [], []
    for sl in tiles:
        kk = kk_raw[:, sl]
        kk = kk * lax.rsqrt(jnp.maximum(head_sum(kk * kk), 1e-12))
        beta = kk * a[:, sl]
        kap.append(stack(kk * w_excl[:, sl]).astype(BF16))
        bh.append(stack(beta * w_inv[:, sl]).astype(BF16))
        kh.append(stack(kt[:, sl] * w_inv[:, sl]).astype(BF16))
        rh32.append(stack(rl[:, sl] * w_incl[:, sl]))
        rh.append(rh32[-1].astype(BF16))
        kw.append(stack(kt[:, sl] * w_rem[:, sl]).astype(BF16))
        bw.append(stack(beta * w_rem[:, sl]).astype(BF16))
        vs.append(stack(vl[:, sl]).astype(BF16))
    n = range(len(tiles))

    a_kb = [_dot_nt(kap[j], bh[j]) for j in n]
    a_kk = [jnp.where(earlier, _dot_nt(kap[j], kh[j]), 0.0).astype(BF16) for j in n]
    a_rb = [jnp.where(earlier_eq, _dot_nt(rh[j], bh[j]), 0.0).astype(BF16) for j in n]
    a_rk = [jnp.where(earlier_eq, _dot_nt(rh[j], kh[j]), 0.0).astype(BF16) for j in n]
    rhs = [_dot(a_kk[j], vs[j]).astype(BF16) for j in n]

    inv = [eye - jnp.where(sibling[0], a_kb[j], 0.0) for j in n]
    for lb in range(1, levels):
        inv_b = [inv[j].astype(BF16) for j in n]
        half = [_dot(inv_b[j], jnp.where(sibling[lb], a_kb[j], 0.0).astype(BF16)).astype(BF16) for j in n]
        inv = [inv[j] - _dot(half[j], inv_b[j]) for j in n]
    inv_b = [inv[j].astype(BF16) for j in n]

    p_mat = [_dot(inv_b[j], kap[j]).astype(BF16) for j in n]
    q_mat = [_dot(inv_b[j], rhs[j]).astype(BF16) for j in n]
    r_eff = [(rh32[j] - _dot(a_rb[j], p_mat[j])).astype(BF16) for j in n]
    y0 = [_dot(a_rk[j], vs[j]) - _dot(a_rb[j], q_mat[j]) for j in n]
    bp = [_dot_tn(bw[j], p_mat[j]).astype(BF16) for j in n]
    h_add = [_dot_tn(kw[j], vs[j]) - _dot_tn(bw[j], q_mat[j]) for j in n]

    h0 = [state_ref[j] for j in n]
    h0_b = [h0[j].astype(BF16) for j in n]
    ys = [_dot(r_eff[j], h0_b[j]) + y0[j] for j in n]
    for j, sl in enumerate(tiles):
        w_col = jnp.sum(jnp.where(lane_eye, w_tot[:, sl], 0.0), axis=1, keepdims=True)
        state_ref[j] = h0[j] * w_col - _dot(bp[j], h0_b[j]) + h_add[j]

    for j, sl in enumerate(tiles):
        y = ys[j][:chunk] + ys[j][chunk:]
        d = y - head_sum(y) * (1.0 / HEAD)
        var = head_sum(d * d) * (1.0 / HEAD)
        out = d * lax.rsqrt(var + GN_EPS) * vec(_LN_W)[:, sl] + vec(_LN_B)[:, sl]
        out = out + head_sum(rl[:, sl] * kt[:, sl] * vec(_R_K)[:, sl]) * vl[:, sl]
        if reverse:
            o_ref[:, sl] = ((fwd_ref[:, sl] + out) * _silu(z_ref[:, sl])).astype(BF16)
        else:
            o_ref[:, sl] = out


def _rwkv_dir(reverse, p_main, p_lora, vecs, mu_lora, w_up, a_up, fwd_out, batch, seq_all, ctx_len,
              col_r, col_k, col_v, col_z, gw):
    m = p_main.shape[0]
    chunk = RW_CHUNK
    assert ctx_len % chunk == 0 and seq_all % chunk == 0
    n_chunks = seq_all // chunk
    n_ctx = ctx_len // chunk
    per8 = chunk // SUBLANES
    last8 = m // SUBLANES - 1
    nl = p_lora.shape[1]

    def chunk_of(i):
        if reverse:
            return jnp.where(i < n_ctx, n_ctx - 1 - i, n_chunks - 1 + n_ctx - i)
        return i

    def blk(b, i):
        return b * n_chunks + chunk_of(i)

    def halo(b, i):
        if reverse:
            return jnp.minimum((blk(b, i) + 1) * per8, last8)
        return jnp.maximum(blk(b, i) * per8 - 1, 0)

    tok = lambda col: pl.BlockSpec((chunk, gw), lambda b, i: (blk(b, i), col))
    hal = lambda col: pl.BlockSpec((SUBLANES, gw), lambda b, i: (halo(b, i), col))
    const = lambda arr: pl.BlockSpec(arr.shape, lambda b, i: (0,) * arr.ndim)
    in_specs = [tok(col_r), tok(col_k), tok(col_v), pl.BlockSpec((chunk, nl), lambda b, i: (blk(b, i), 0)),
                hal(col_r), hal(col_k), hal(col_v), pl.BlockSpec((SUBLANES, nl), lambda b, i: (halo(b, i), 0)),
                const(vecs), const(mu_lora), const(w_up), const(a_up)]
    args = [p_main, p_main, p_main, p_lora, p_main, p_main, p_main, p_lora, vecs, mu_lora, w_up, a_up]
    if reverse:
        in_specs += [pl.BlockSpec((chunk, gw), lambda b, i: (blk(b, i), 0)), tok(col_z)]
        args += [fwd_out, p_main]
    return pl.pallas_call(
        functools.partial(_rwkv_kernel, reverse, chunk, n_ctx),
        grid=(batch, n_chunks),
        in_specs=in_specs,
        out_specs=pl.BlockSpec((chunk, gw), lambda b, i: (blk(b, i), 0)),
        out_shape=jax.ShapeDtypeStruct((m, gw), BF16 if reverse else F32),
        scratch_shapes=[pltpu.VMEM((gw // LANES, LANES, LANES), F32)],
        compiler_params=_params("parallel", "arbitrary"),
        name="rwkv_rev" if reverse else "rwkv_fwd",
    )(*args)


def _rwkv(p_main, p_lora, rw, batch, seq_all, ctx_len, cols, gw):
    mu_rkv, mu_lora, w0, w_up, a0, a_up, k_k, k_a, r_k, ln_w, ln_b = rw
    out = None
    for d, reverse in enumerate((False, True)):
        rows = [mu_rkv[d, 0], mu_rkv[d, 1], mu_rkv[d, 2], w0[d], a0[d], k_k, k_a, r_k.reshape(-1), ln_w, ln_b]
        vecs = jnp.zeros((2 * SUBLANES, gw), F32).at[:len(rows)].set(jnp.stack(rows))
        zeros = jnp.zeros((RW_LORA, gw), F32)
        parts_w = [zeros] * 4
        parts_a = [zeros] * 4
        parts_w[d] = w_up[d]
        parts_a[2 + d] = a_up[d]
        mu = jnp.zeros((4, RW_LORA), F32).at[d].set(mu_lora[d, 0]).at[2 + d].set(mu_lora[d, 1])
        out = _rwkv_dir(reverse, p_main, p_lora, vecs, mu.reshape(1, 4 * RW_LORA),
                        jnp.concatenate(parts_w).astype(BF16), jnp.concatenate(parts_a).astype(BF16),
                        out, batch, seq_all, ctx_len, *cols, gw)
    return out


def _merge_kernel(tm, nb, ctx_len, y0_ref, y1_ref, y2_ref, y3_ref, w_ref, h_ref, gb_ref, gc_ref, o_ref):
    gw = y0_ref.shape[1]
    acc = _dot(y0_ref[...], w_ref[0:gw, :])
    for g, y_ref in enumerate((y1_ref, y2_ref, y3_ref), start=1):
        acc += _dot(y_ref[...], w_ref[g * gw:(g + 1) * gw, :])
    pos = (pl.program_id(0) % nb) * tm + _iota((tm, 1), 0)
    gate = jnp.where(pos < ctx_len, gc_ref[0], gb_ref[0])
    o_ref[...] = h_ref[...] + gate * acc


def _merge(ys, w_out, h, mod3, batch, seq_all, ctx_len, tm, tn):
    m, d = h.shape
    gw = ys[0].shape[1]
    nb = seq_all // tm
    y_spec = pl.BlockSpec((tm, gw), lambda i, j: (i, 0))
    gate_col = 2 * (d // tn)
    return pl.pallas_call(
        functools.partial(_merge_kernel, tm, nb, ctx_len),
        grid=(m // tm, d // tn),
        in_specs=[y_spec, y_spec, y_spec, y_spec,
                  pl.BlockSpec((w_out.shape[0], tn), lambda i, j: (0, j)),
                  pl.BlockSpec((tm, tn), lambda i, j: (i, j)),
                  pl.BlockSpec((1, 1, tn), lambda i, j: (i // nb, 0, gate_col + j)),
                  pl.BlockSpec((1, 1, tn), lambda i, j: (batch, 0, gate_col + j))],
        out_specs=pl.BlockSpec((tm, tn), lambda i, j: (i, j)),
        out_shape=jax.ShapeDtypeStruct((m, d), F32),
        compiler_params=_params("parallel", "parallel"),
        name="merge",
    )(*ys, w_out, h, mod3, mod3)


def kernel(x, c, ctx, c_ctx, w_mod, b_mod, w_in, w_out, pool_w, pool_scale, na_q_gain, na_k_gain, na_rpb, rw_mu_rkv, rw_mu_lora, rw_w0, rw_w_up, rw_a0, rw_a_up, rw_k_k, rw_k_a, rw_r_k, rw_ln_w, rw_ln_b, sg_ln_w, sg_ln_b, sg_w, sg_b):
    batch, seq, d = x.shape
    ctx_len = ctx.shape[1]
    seq_all = ctx_len + seq
    m = batch * seq_all
    depth = w_mod.shape[0]
    gw = d // 4
    assert batch < SUBLANES and gw % LANES == 0

    h = jnp.concatenate([ctx, x], axis=1).reshape(m, d)
    cond = jnp.zeros((SUBLANES, d), F32).at[:batch].set(c).at[batch].set(c_ctx)
    mod = _modulation(cond, w_mod, b_mod)
    cos, sin = _rope_tables(seq, ctx_len)
    tm = _largest_tile(seq_all, 2 * LANES, 1024)

    lora_lo = 10 * gw
    lora_hi = lora_lo + 4 * RW_LORA
    (c_pool_v, c_pool_z, c_na_q, c_na_k, c_na_v, c_na_z, c_rw_r, c_rw_k, c_rw_v, c_rw_z,
     c_sg_u, c_sg_v, c_sg_z) = range(13)

    for l in range(depth):
        w_main = jnp.concatenate([w_in[l, :, :lora_lo], w_in[l, :, lora_hi:]], axis=1).astype(BF16)
        w_lora = w_in[l, :, lora_lo:lora_hi].astype(BF16)
        mod3 = mod[l].reshape(SUBLANES, 1, 3 * d)
        xn = _norm_mod(h, mod3, batch, seq_all, ctx_len)
        p_main = _matmul(xn, w_main, tm, gw)
        p_lora = _matmul(xn, w_lora, tm, 4 * RW_LORA)

        y_pool = _pool(p_main, pool_w[l].astype(BF16), pool_scale[l], seq_all, ctx_len, c_pool_v, c_pool_z)
        qn, kn, vb = _na_prep(p_main, cos, sin, na_q_gain[l], na_k_gain[l], seq_all, ctx_len,
                              c_na_q, c_na_k, c_na_v, gw)
        y_na = _na(qn, kn, vb, p_main, _na_bias(na_rpb[l]), batch, seq, ctx_len, c_na_z)
        rw = (rw_mu_rkv[l], rw_mu_lora[l], rw_w0[l], rw_w_up[l], rw_a0[l], rw_a_up[l],
              rw_k_k[l], rw_k_a[l], rw_r_k[l], rw_ln_w[l], rw_ln_b[l])
        y_rw = _rwkv(p_main, p_lora, rw, batch, seq_all, ctx_len, (c_rw_r, c_rw_k, c_rw_v, c_rw_z), gw)
        y_sg = _sgu(p_main, sg_ln_w[l], sg_ln_b[l], sg_w[l], sg_b[l], c_sg_u, c_sg_v, c_sg_z)
        h = _merge((y_pool, y_na, y_rw, y_sg), w_out[l].astype(BF16), h, mod3, batch, seq_all, ctx_len, tm, gw)

    return h.reshape(batch, seq_all, d)[:, ctx_len:, :]
```

```python
import functools

import jax
import jax.numpy as jnp
from jax import lax
from jax.experimental import pallas as pl
from jax.experimental.pallas import tpu as pltpu

F32 = jnp.float32
BF16 = jnp.bfloat16

LANES = 128
SUBLANES = 8
HEAD = 64
GRID_W = 64
POOL_WINDOWS = (2, 4, 8, 16)
NA_KH = 8
NA_KW = 16
ROPE_BASE = 10000.0
RW_LORA = 64
RW_CHUNK = 64
RW_BLOCK_CHUNKS = 2
GN_EPS = 64e-5
SGU_CHUNK = 128
SGU_GROUPS = 8
LN_EPS = 1e-5
NORM_EPS = 1e-6
MASKED = -1e30
VMEM_LIMIT = 56 * 1024 * 1024


def _dot(a, b):
    return jnp.dot(a, b, preferred_element_type=F32)


def _dot_nt(a, b):
    return lax.dot_general(a, b, (((1,), (1,)), ((), ())), preferred_element_type=F32)


def _dot_tn(a, b):
    return lax.dot_general(a, b, (((0,), (0,)), ((), ())), preferred_element_type=F32)


def _silu(x):
    return x * jax.nn.sigmoid(x)


def _split3(x):
    hi = x.astype(BF16)
    r = x - hi.astype(F32)
    mid = r.astype(BF16)
    lo = (r - mid.astype(F32)).astype(BF16)
    return hi, mid, lo


def _iota(shape, dim):
    return lax.broadcasted_iota(jnp.int32, shape, dim)


def _params(*sem):
    return pltpu.CompilerParams(dimension_semantics=sem, vmem_limit_bytes=VMEM_LIMIT)


def _largest_tile(n, unit, cap):
    best = unit
    for t in range(unit, cap + 1, unit):
        if n % t == 0:
            best = t
    return best


def _mod_kernel(c_ref, w_ref, b_ref, o_ref):
    cs = _silu(c_ref[...]).astype(BF16)
    o_ref[0] = _dot(cs, w_ref[0].astype(BF16)) + b_ref[0]


def _modulation(cond, w_mod, b_mod):
    depth, d, n = w_mod.shape
    tn = 512
    return pl.pallas_call(
        _mod_kernel,
        grid=(depth, n // tn),
        in_specs=[
            pl.BlockSpec((SUBLANES, d), lambda l, j: (0, 0)),
            pl.BlockSpec((1, d, tn), lambda l, j: (l, 0, j)),
            pl.BlockSpec((1, 1, tn), lambda l, j: (l, 0, j)),
        ],
        out_specs=pl.BlockSpec((1, SUBLANES, tn), lambda l, j: (l, 0, j)),
        out_shape=jax.ShapeDtypeStruct((depth, SUBLANES, n), F32),
        compiler_params=_params("parallel", "parallel"),
        name="modulation",
    )(cond, w_mod, b_mod.reshape(depth, 1, n))


def _norm_kernel(h_ref, shift_ref, scale_ref, o_ref):
    x = h_ref[...]
    xn = x * lax.rsqrt(jnp.mean(x * x, axis=-1, keepdims=True) + NORM_EPS)
    o_ref[...] = (xn * (1.0 + scale_ref[0]) + shift_ref[0]).astype(BF16)


def _norm_mod(h, mod3, batch, seq_all, ctx_len):
    m, d = h.shape
    tb = _largest_tile(ctx_len, SUBLANES, 256)
    nb = seq_all // tb

    def row(i):
        return jnp.where(i % nb < ctx_len // tb, batch, i // nb)

    return pl.pallas_call(
        _norm_kernel,
        grid=(m // tb,),
        in_specs=[
            pl.BlockSpec((tb, d), lambda i: (i, 0)),
            pl.BlockSpec((1, 1, d), lambda i: (row(i), 0, 0)),
            pl.BlockSpec((1, 1, d), lambda i: (row(i), 0, 1)),
        ],
        out_specs=pl.BlockSpec((tb, d), lambda i: (i, 0)),
        out_shape=jax.ShapeDtypeStruct((m, d), BF16),
        compiler_params=_params("parallel"),
        name="norm_mod",
    )(h, mod3, mod3)


def _mm_kernel(a_ref, w_ref, o_ref):
    o_ref[...] = _dot(a_ref[...], w_ref[0])


def _in_proj(a, w_all, layer, tm, tn, n_out, col_start):
    m, k = a.shape
    w_spec = pl.BlockSpec((pl.Element(1), pl.Element(k), pl.Element(tn)),
                          lambda i, j: (layer, 0, pl.multiple_of(col_start(j), LANES)))
    return pl.pallas_call(
        _mm_kernel,
        grid=(m // tm, n_out // tn),
        in_specs=[pl.BlockSpec((tm, k), lambda i, j: (i, 0)), w_spec],
        out_specs=pl.BlockSpec((tm, tn), lambda i, j: (i, j)),
        out_shape=jax.ShapeDtypeStruct((m, n_out), F32),
        compiler_params=_params("parallel", "parallel"),
        name="in_proj",
    )(a, w_all)


def _pool_kernel(tb, nb, ctx_len, seq_all, v_ref, vp_ref, vn_ref, z_ref, pw_ref, ps_ref, o_ref):
    s0 = (pl.program_id(0) % nb) * tb
    is_ctx = s0 < ctx_len
    seq_lo = jnp.where(is_ctx, 0, ctx_len) - s0
    seq_hi = jnp.where(is_ctx, ctx_len, seq_all) - s0
    v = v_ref[...]
    ext = jnp.concatenate([vp_ref[...], v, vn_ref[...]], axis=0)
    halo = SUBLANES
    t = _iota((tb, tb + 2 * halo), 0)
    e = _iota((tb, tb + 2 * halo), 1) - halo
    t1 = _iota((tb, 1), 0)
    pool_ch = v.shape[1] // len(POOL_WINDOWS)
    outs = []
    for g, win in enumerate(POOL_WINDOWS):
        lo = jnp.maximum(t - win // 2, seq_lo)
        hi = jnp.minimum(t - win // 2 + win, seq_hi)
        band = jnp.where((e >= lo) & (e < hi), 1.0, 0.0).astype(BF16)
        cnt = (jnp.minimum(t1 - win // 2 + win, seq_hi) - jnp.maximum(t1 - win // 2, seq_lo)).astype(F32)
        ch = slice(g * pool_ch, (g + 1) * pool_ch)
        p_hi, p_mid, p_lo = _split3(ext[:, ch])
        mean = (_dot(band, p_hi) + _dot(band, p_mid) + _dot(band, p_lo)) / cnt
        outs.append(_dot((mean - v[:, ch]).astype(BF16), pw_ref[g]))
    y = jnp.concatenate(outs, axis=1) * ps_ref[...]
    o_ref[...] = (y * _silu(z_ref[...])).astype(BF16)


def _pool(p_main, pool_w, pool_scale, seq_all, ctx_len, col_v, col_z):
    m = p_main.shape[0]
    gw = pool_scale.shape[-1]
    tb = _largest_tile(ctx_len, SUBLANES, 256)
    nb = seq_all // tb
    hb = tb // SUBLANES
    last = m // SUBLANES - 1
    return pl.pallas_call(
        functools.partial(_pool_kernel, tb, nb, ctx_len, seq_all),
        grid=(m // tb,),
        in_specs=[
            pl.BlockSpec((tb, gw), lambda i: (i, col_v)),
            pl.BlockSpec((SUBLANES, gw), lambda i: (jnp.maximum(i * hb - 1, 0), col_v)),
            pl.BlockSpec((SUBLANES, gw), lambda i: (jnp.minimum((i + 1) * hb, last), col_v)),
            pl.BlockSpec((tb, gw), lambda i: (i, col_z)),
            pl.BlockSpec(pool_w.shape, lambda i: (0, 0, 0)),
            pl.BlockSpec((1, gw), lambda i: (0, 0)),
        ],
        out_specs=pl.BlockSpec((tb, gw), lambda i: (i, 0)),
        out_shape=jax.ShapeDtypeStruct((m, gw), BF16),
        compiler_params=_params("parallel"),
        name="pool",
    )(p_main, p_main, p_main, p_main, pool_w, pool_scale.reshape(1, gw))


def _sgu_kernel(u_ref, v_ref, z_ref, lnw_ref, lnb_ref, ws_ref, bias_ref, o_ref):
    v = v_ref[...]
    d = v - jnp.mean(v, axis=-1, keepdims=True)
    var = jnp.mean(d * d, axis=-1, keepdims=True)
    vn = (d * lax.rsqrt(var + LN_EPS) * lnw_ref[...] + lnb_ref[...]).astype(BF16)
    ch = v.shape[1] // SGU_GROUPS
    mixed = jnp.concatenate(
        [_dot(ws_ref[g], vn[:, g * ch:(g + 1) * ch]) for g in range(SGU_GROUPS)], axis=1)
    o_ref[...] = (u_ref[...] * (mixed + bias_ref[...]) * _silu(z_ref[...])).astype(BF16)


def _sgu(p_main, ln_w, ln_b, w_s, b_s, col_u, col_v, col_z):
    m = p_main.shape[0]
    gw = ln_w.shape[-1]
    bias = jnp.repeat(b_s.T, gw // SGU_GROUPS, axis=1)
    tok = lambda col: pl.BlockSpec((SGU_CHUNK, gw), lambda i: (i, col))
    vec = pl.BlockSpec((1, gw), lambda i: (0, 0))
    return pl.pallas_call(
        _sgu_kernel,
        grid=(m // SGU_CHUNK,),
        in_specs=[tok(col_u), tok(col_v), tok(col_z), vec, vec,
                  pl.BlockSpec(w_s.shape, lambda i: (0, 0, 0)),
                  pl.BlockSpec((SGU_CHUNK, gw), lambda i: (0, 0))],
        out_specs=pl.BlockSpec((SGU_CHUNK, gw), lambda i: (i, 0)),
        out_shape=jax.ShapeDtypeStruct((m, gw), BF16),
        compiler_params=_params("parallel"),
        name="sgu",
    )(p_main, p_main, p_main, ln_w.reshape(1, gw), ln_b.reshape(1, gw), w_s.astype(BF16), bias)


def _na_prep_kernel(q_ref, k_ref, v_ref, cos_ref, sin_ref, qg_ref, kg_ref, ones_ref, qo_ref, ko_ref, vo_ref):
    cos = cos_ref[...]
    sin = sin_ref[...]
    ones = ones_ref[...]
    quarter = HEAD // 4
    low_quarter = (_iota((1, LANES), 1) % (2 * quarter)) < quarter

    def prep(x_ref, gain, scale, o_ref):
        for j in range(x_ref.shape[1] // LANES):
            sl = slice(j * LANES, (j + 1) * LANES)
            x = x_ref[:, sl]
            sq = x * x
            sq_hi = sq.astype(BF16)
            sq_lo = (sq - sq_hi.astype(F32)).astype(BF16)
            ms = (_dot(sq_hi, ones) + _dot(sq_lo, ones)) * (1.0 / HEAD)
            xn = x * lax.rsqrt(ms + NORM_EPS) * gain
            partner = jnp.where(low_quarter, pltpu.roll(xn, LANES - quarter, 1), pltpu.roll(xn, quarter, 1))
            o_ref[:, sl] = ((xn * cos + partner * sin) * scale).astype(BF16)

    prep(q_ref, qg_ref[...], HEAD ** -0.5, qo_ref)
    prep(k_ref, kg_ref[...], 1.0, ko_ref)
    vo_ref[...] = v_ref[...].astype(BF16)


def _rope_tables(seq, ctx_len):
    t = jnp.arange(seq)
    n_freq = HEAD // 4
    inv = ROPE_BASE ** (-jnp.arange(n_freq, dtype=F32) / n_freq)
    ang_r = (t // GRID_W).astype(F32)[:, None] * inv
    ang_c = (t % GRID_W).astype(F32)[:, None] * inv
    cos = jnp.concatenate([jnp.cos(ang_r), jnp.cos(ang_r), jnp.cos(ang_c), jnp.cos(ang_c)], axis=1)
    sin = jnp.concatenate([-jnp.sin(ang_r), jnp.sin(ang_r), -jnp.sin(ang_c), jnp.sin(ang_c)], axis=1)
    cos = jnp.concatenate([jnp.ones((ctx_len, HEAD), F32), cos], axis=0)
    sin = jnp.concatenate([jnp.zeros((ctx_len, HEAD), F32), sin], axis=0)
    reps = LANES // HEAD
    return jnp.tile(cos, (1, reps)), jnp.tile(sin, (1, reps))


def _na_prep(p_main, cos, sin, q_gain, k_gain, seq_all, ctx_len, col_q, col_k, col_v, gw):
    m = p_main.shape[0]
    tb = _largest_tile(ctx_len, SUBLANES, 256)
    nb = seq_all // tb
    reps = LANES // HEAD
    lane_head = jnp.arange(LANES) // HEAD
    ones = (lane_head[:, None] == lane_head[None, :]).astype(BF16)
    tok = lambda col: pl.BlockSpec((tb, gw), lambda i: (i, col))
    tab = pl.BlockSpec((tb, LANES), lambda i: (i % nb, 0))
    vec = pl.BlockSpec((1, LANES), lambda i: (0, 0))
    out = jax.ShapeDtypeStruct((m, gw), BF16)
    return pl.pallas_call(
        _na_prep_kernel,
        grid=(m // tb,),
        in_specs=[tok(col_q), tok(col_k), tok(col_v), tab, tab, vec, vec,
                  pl.BlockSpec((LANES, LANES), lambda i: (0, 0))],
        out_specs=[pl.BlockSpec((tb, gw), lambda i: (i, 0))] * 3,
        out_shape=[out, out, out],
        compiler_params=_params("parallel"),
        name="na_prep",
    )(p_main, p_main, p_main, cos, sin, jnp.tile(q_gain, reps).reshape(1, LANES),
      jnp.tile(k_gain, reps).reshape(1, LANES), ones)


def _na_bias(rpb):
    d = jnp.arange(NA_KH)[:, None]
    i = jnp.arange(NA_KH)[None, :]
    rel_r = i - d + NA_KH - 1
    qc = jnp.arange(GRID_W)[:, None]
    kc = jnp.arange(GRID_W)[None, :]
    cs = jnp.clip(qc - NA_KW // 2, 0, GRID_W - NA_KW)
    valid = (kc >= cs) & (kc < cs + NA_KW)
    rel_c = jnp.clip(kc - qc + NA_KW - 1, 0, 2 * NA_KW - 2)
    t = rpb[:, rel_r][:, :, :, rel_c]
    t = jnp.where(valid, t, MASKED)
    heads = rpb.shape[0]
    reps = LANES // HEAD
    t = t.transpose(1, 0, 3, 2, 4).reshape(NA_KH, heads // reps, reps * GRID_W, NA_KH * GRID_W)
    return jnp.concatenate([t, jnp.full((1,) + t.shape[1:], MASKED, F32)], axis=0)


def _na_kernel(q_ref, kw_ref, vw_ref, kc_ref, vc_ref, bias_ref, z_ref, o_ref):
    head0 = _iota((1, LANES), 1) < HEAD
    rows = q_ref.shape[0]
    tiles = [slice(j * LANES, (j + 1) * LANES) for j in range(q_ref.shape[1] // LANES)]
    n = range(len(tiles))
    qs = []
    for sl in tiles:
        q = q_ref[:, sl]
        zero = jnp.zeros_like(q)
        qs.append(jnp.concatenate([jnp.where(head0, q, zero), jnp.where(head0, zero, q)], axis=0))
    s_w = [_dot_nt(qs[j], kw_ref[:, tiles[j]]) + bias_ref[0, j] for j in n]
    s_c = [_dot_nt(qs[j], kc_ref[:, tiles[j]]) for j in n]
    mx = [jnp.maximum(jnp.max(s_w[j], axis=-1, keepdims=True), jnp.max(s_c[j], axis=-1, keepdims=True)) for j in n]
    p_w = [jnp.exp(s_w[j] - mx[j]) for j in n]
    p_c = [jnp.exp(s_c[j] - mx[j]) for j in n]
    den = [jnp.sum(p_w[j], axis=-1, keepdims=True) + jnp.sum(p_c[j], axis=-1, keepdims=True) for j in n]
    o = [_dot(p_w[j].astype(BF16), vw_ref[:, tiles[j]]) + _dot(p_c[j].astype(BF16), vc_ref[:, tiles[j]]) for j in n]
    for j, sl in enumerate(tiles):
        oj = o[j] / den[j]
        y = jnp.where(head0, oj[:rows], oj[rows:])
        o_ref[:, sl] = (y * _silu(z_ref[:, sl])).astype(BF16)


def _na(qn, kn, vb, p_main, bias, batch, seq, ctx_len, col_z):
    m, gw = qn.shape
    seq_all = seq + ctx_len
    rows = seq // GRID_W
    assert rows >= NA_KH and ctx_len % GRID_W == 0 and seq_all % ctx_len == 0
    win = NA_KH * GRID_W

    def q_blk(b, r):
        return jnp.where(r < rows, (b * seq_all + ctx_len) // GRID_W + r, (b * seq_all) // GRID_W + r - rows)

    def win_start(b, r):
        start = b * seq_all + ctx_len + jnp.clip(r - NA_KH // 2, 0, rows - NA_KH) * GRID_W
        return pl.multiple_of(start, GRID_W)

    def bias_blk(b, r):
        return jnp.where(r < rows, r - jnp.clip(r - NA_KH // 2, 0, rows - NA_KH), NA_KH)

    window = pl.BlockSpec((pl.Element(win), pl.Element(gw)), lambda b, r: (win_start(b, r), 0))
    ctx_kv = pl.BlockSpec((ctx_len, gw), lambda b, r: (b * (seq_all // ctx_len), 0))
    return pl.pallas_call(
        _na_kernel,
        grid=(batch, rows + ctx_len // GRID_W),
        in_specs=[
            pl.BlockSpec((GRID_W, gw), lambda b, r: (q_blk(b, r), 0)),
            window, window, ctx_kv, ctx_kv,
            pl.BlockSpec((1,) + bias.shape[1:], lambda b, r: (bias_blk(b, r), 0, 0, 0)),
            pl.BlockSpec((GRID_W, gw), lambda b, r: (q_blk(b, r), col_z)),
        ],
        out_specs=pl.BlockSpec((GRID_W, gw), lambda b, r: (q_blk(b, r), 0)),
        out_shape=jax.ShapeDtypeStruct((m, gw), BF16),
        compiler_params=_params("parallel", "arbitrary"),
        name="na_attention",
    )(qn, kn, vb, kn, vb, bias, p_main)


(_MU_R, _MU_K, _MU_V, _W0, _A0, _K_K, _K_A, _R_K, _LN_W, _LN_B) = range(10)


def _rwkv_kernel(reverse, chunk, n_ctx_blocks, r_ref, k_ref, v_ref, lo_ref, rh_ref, kh_ref, vh_ref, loh_ref,
                 vec_ref, mul_ref, wup_ref, aup_ref, *rest):
    if reverse:
        fwd_ref, z_ref, o_ref, state_ref = rest
    else:
        o_ref, state_ref = rest
    step = pl.program_id(1)
    rows = r_ref.shape[0]
    subs = rows // chunk

    @pl.when(step == 0)
    def _():
        state_ref[...] = jnp.zeros_like(state_ref)

    seq_start = jnp.logical_or(step == 0, step == n_ctx_blocks)
    row_id = _iota((rows, 1), 0)

    def lerp_shift(x_ref, halo_ref, mu):
        x = x_ref[...]
        if reverse:
            edge, rolled, edge_row = halo_ref[0:1, :], pltpu.roll(x, rows - 1, 0), rows - 1
        else:
            edge, rolled, edge_row = halo_ref[SUBLANES - 1:SUBLANES, :], pltpu.roll(x, 1, 0), 0
        edge = jnp.where(seq_start, jnp.zeros_like(edge), edge)
        return x + (jnp.where(row_id == edge_row, edge, rolled) - x) * mu

    vec = lambda n: vec_ref[n:n + 1, :]
    rl = lerp_shift(r_ref, rh_ref, vec(_MU_R))
    kl = lerp_shift(k_ref, kh_ref, vec(_MU_K))
    vl = lerp_shift(v_ref, vh_ref, vec(_MU_V))
    lol = lerp_shift(lo_ref, loh_ref, mul_ref[...])
    neg_w_pre = -(vec(_W0) + _dot(jnp.tanh(lol).astype(BF16), wup_ref[...]))
    w = -(jnp.maximum(neg_w_pre, 0.0) + jnp.log1p(jnp.exp(-jnp.abs(neg_w_pre)))) - 0.5
    ew = jnp.exp(w)
    a = jax.nn.sigmoid(vec(_A0) + _dot(lol.astype(BF16), aup_ref[...]))

    tt = _iota((rows, rows), 0)
    ss = _iota((rows, rows), 1)
    same_chunk = (tt ^ ss) < chunk
    tri = jnp.where(same_chunk & ((ss >= tt) if reverse else (ss <= tt)), 1.0, 0.0).astype(BF16)
    e_hi, e_mid, e_lo = _split3(ew)
    cs = _dot(tri, e_hi) + _dot(tri, e_mid) + _dot(tri, e_lo)
    kk_raw = kl * vec(_K_K)
    kt = kl * (1.0 + (a - 1.0) * vec(_K_A))
    w_incl = jnp.exp(-cs)
    w_excl = jnp.exp(ew - cs)
    w_inv = jnp.exp(cs)

    head0 = _iota((1, LANES), 1) < HEAD
    t_loc = _iota((chunk, LANES), 0)
    s_loc = _iota((chunk, LANES), 1) & (chunk - 1)
    earlier = (s_loc > t_loc) if reverse else (s_loc < t_loc)
    earlier_eq = (s_loc >= t_loc) if reverse else (s_loc <= t_loc)
    levels = chunk.bit_length() - 1
    later_half = 0 if reverse else 1
    sibling = [(((t_loc >> lb) ^ (s_loc >> lb)) == 1) & (((t_loc >> lb) & 1) == later_half)
               for lb in range(levels)]
    eye = jnp.where(s_loc == t_loc, 1.0, 0.0)
    lane_r = _iota((LANES, LANES), 0)
    lane_c = _iota((LANES, LANES), 1)
    lane_eye = lane_r == lane_c
    same_head = (lane_r ^ lane_c) < HEAD

    def stack(x):
        z = jnp.zeros_like(x)
        return jnp.concatenate([jnp.where(head0, x, z), jnp.where(head0, z, x)], axis=0)

    def head_sum(x):
        s0 = jnp.sum(jnp.where(head0, x, 0.0), axis=-1, keepdims=True)
        s1 = jnp.sum(jnp.where(head0, 0.0, x), axis=-1, keepdims=True)
        return jnp.where(head0, s0, s1)

    tiles = [slice(j * LANES, (j + 1) * LANES) for j in range(r_ref.shape[1] // LANES)]
    units = [(slice(u * chunk, (u + 1) * chunk), sl) for u in range(subs) for sl in tiles]
    kap, bhkh, rh32, rh, kw, bw, vb, vs, w_tot = [], [], [], [], [], [], [], [], []
    for ru, sl in units:
        kk = kk_raw[ru, sl]
        kk = kk * lax.rsqrt(jnp.maximum(head_sum(kk * kk), 1e-12))
        beta = kk * a[ru, sl]
        cs_u = cs[ru, sl]
        tot = cs_u[0:1, :] if reverse else cs_u[chunk - 1:chunk, :]
        w_rem = jnp.exp(cs_u - tot)
        w_tot.append(jnp.exp(-tot))
        kap.append((kk * w_excl[ru, sl]).astype(BF16))
        bhkh.append(jnp.concatenate([stack(beta * w_inv[ru, sl]), stack(kt[ru, sl] * w_inv[ru, sl])],
                                    axis=0).astype(BF16))
        rh32.append(rl[ru, sl] * w_incl[ru, sl])
        rh.append(rh32[-1].astype(BF16))
        kw.append((kt[ru, sl] * w_rem).astype(BF16))
        bw.append((beta * w_rem).astype(BF16))
        vb.append(vl[ru, sl].astype(BF16))
        vs.append(stack(vb[-1]))
    n = range(len(units))

    a_k = [_dot_nt(kap[j], bhkh[j]) for j in n]
    a_r = [_dot_nt(rh[j], bhkh[j]) for j in n]
    a_kb = [a_k[j][:, :LANES] for j in n]
    a_kk = [jnp.where(earlier, a_k[j][:, LANES:], 0.0).astype(BF16) for j in n]
    a_rb = [jnp.where(earlier_eq, a_r[j][:, :LANES], 0.0).astype(BF16) for j in n]
    a_rk = [jnp.where(earlier_eq, a_r[j][:, LANES:], 0.0).astype(BF16) for j in n]
    rhs = [_dot(a_kk[j], vs[j]).astype(BF16) for j in n]

    inv = [eye - jnp.where(sibling[0], a_kb[j], 0.0) for j in n]
    for lb in range(1, levels):
        inv_b = [inv[j].astype(BF16) for j in n]
        half = [_dot(inv_b[j], stack(jnp.where(sibling[lb], a_kb[j], 0.0).astype(BF16))).astype(BF16) for j in n]
        inv = [inv[j] - _dot(half[j], stack(inv_b[j])) for j in n]
    inv_b = [inv[j].astype(BF16) for j in n]

    pq = [_dot(inv_b[j], jnp.concatenate([stack(kap[j]), stack(rhs[j])], axis=1)) for j in n]
    p_mat = [pq[j][:, :LANES].astype(BF16) for j in n]
    q_mat = [pq[j][:, LANES:].astype(BF16) for j in n]
    rb_pq = [_dot(a_rb[j], jnp.concatenate([stack(p_mat[j]), stack(q_mat[j])], axis=1)) for j in n]
    r_eff = [(rh32[j] - rb_pq[j][:, :LANES]).astype(BF16) for j in n]
    y0 = [_dot(a_rk[j], vs[j]) - rb_pq[j][:, LANES:] for j in n]
    bw_pq = [_dot_tn(bw[j], jnp.concatenate([p_mat[j], q_mat[j]], axis=1)) for j in n]
    bp = [jnp.where(same_head, bw_pq[j][:, :LANES], 0.0).astype(BF16) for j in n]
    h_add = [jnp.where(same_head, _dot_tn(kw[j], vb[j]) - bw_pq[j][:, LANES:], 0.0) for j in n]

    w_col = [jnp.sum(jnp.where(lane_eye, w_tot[i], 0.0), axis=1, keepdims=True) for i in n]
    state = [state_ref[j] for j in range(len(tiles))]
    ys = [None] * len(units)
    for u in (range(subs - 1, -1, -1) if reverse else range(subs)):
        for j in range(len(tiles)):
            i = u * len(tiles) + j
            h_b = state[j].astype(BF16)
            ys[i] = _dot(r_eff[i], h_b) + y0[i]
            state[j] = state[j] * w_col[i] - _dot(bp[i], h_b) + h_add[i]
    for j in range(len(tiles)):
        state_ref[j] = state[j]

    for i, (ru, sl) in enumerate(units):
        y = ys[i]
        d = y - head_sum(y) * (1.0 / HEAD)
        var = head_sum(d * d) * (1.0 / HEAD)
        out = d * lax.rsqrt(var + GN_EPS) * vec(_LN_W)[:, sl] + vec(_LN_B)[:, sl]
        out = out + head_sum(rl[ru, sl] * kt[ru, sl] * vec(_R_K)[:, sl]) * vl[ru, sl]
        if reverse:
            o_ref[ru, sl] = ((fwd_ref[ru, sl] + out) * _silu(z_ref[ru, sl])).astype(BF16)
        else:
            o_ref[ru, sl] = out


def _rwkv_dir(reverse, p_main, p_lora, vecs, mu_lora, w_up, a_up, fwd_out, batch, seq_all, ctx_len,
              col_r, col_k, col_v, col_z, gw):
    m = p_main.shape[0]
    chunk = RW_CHUNK
    assert 2 * chunk == LANES and chunk == HEAD
    rows = RW_BLOCK_CHUNKS * chunk
    assert ctx_len % rows == 0 and seq_all % rows == 0
    n_blocks = seq_all // rows
    n_ctx = ctx_len // rows
    per8 = rows // SUBLANES
    last8 = m // SUBLANES - 1
    nl = p_lora.shape[1]

    def block_of(i):
        if reverse:
            return jnp.where(i < n_ctx, n_ctx - 1 - i, n_blocks - 1 + n_ctx - i)
        return i

    def blk(b, i):
        return b * n_blocks + block_of(i)

    def halo(b, i):
        if reverse:
            return jnp.minimum((blk(b, i) + 1) * per8, last8)
        return jnp.maximum(blk(b, i) * per8 - 1, 0)

    tok = lambda col: pl.BlockSpec((rows, gw), lambda b, i: (blk(b, i), col))
    hal = lambda col: pl.BlockSpec((SUBLANES, gw), lambda b, i: (halo(b, i), col))
    const = lambda arr: pl.BlockSpec(arr.shape, lambda b, i: (0,) * arr.ndim)
    in_specs = [tok(col_r), tok(col_k), tok(col_v), pl.BlockSpec((rows, nl), lambda b, i: (blk(b, i), 0)),
                hal(col_r), hal(col_k), hal(col_v), pl.BlockSpec((SUBLANES, nl), lambda b, i: (halo(b, i), 0)),
                const(vecs), const(mu_lora), const(w_up), const(a_up)]
    args = [p_main, p_main, p_main, p_lora, p_main, p_main, p_main, p_lora, vecs, mu_lora, w_up, a_up]
    if reverse:
        in_specs += [pl.BlockSpec((rows, gw), lambda b, i: (blk(b, i), 0)), tok(col_z)]
        args += [fwd_out, p_main]
    return pl.pallas_call(
        functools.partial(_rwkv_kernel, reverse, chunk, n_ctx),
        grid=(batch, n_blocks),
        in_specs=in_specs,
        out_specs=pl.BlockSpec((rows, gw), lambda b, i: (blk(b, i), 0)),
        out_shape=jax.ShapeDtypeStruct((m, gw), BF16 if reverse else F32),
        scratch_shapes=[pltpu.VMEM((gw // LANES, LANES, LANES), F32)],
        compiler_params=_params("parallel", "arbitrary"),
        name="rwkv_rev" if reverse else "rwkv_fwd",
    )(*args)


def _rwkv(p_main, p_lora, rw, batch, seq_all, ctx_len, cols, gw):
    mu_rkv, mu_lora, w0, w_up, a0, a_up, k_k, k_a, r_k, ln_w, ln_b = rw
    out = None
    for d, reverse in enumerate((False, True)):
        rows = [mu_rkv[d, 0], mu_rkv[d, 1], mu_rkv[d, 2], w0[d], a0[d], k_k, k_a, r_k.reshape(-1), ln_w, ln_b]
        vecs = jnp.zeros((2 * SUBLANES, gw), F32).at[:len(rows)].set(jnp.stack(rows))
        zeros = jnp.zeros((RW_LORA, gw), F32)
        parts_w = [zeros] * 4
        parts_a = [zeros] * 4
        parts_w[d] = w_up[d]
        parts_a[2 + d] = a_up[d]
        mu = jnp.zeros((4, RW_LORA), F32).at[d].set(mu_lora[d, 0]).at[2 + d].set(mu_lora[d, 1])
        out = _rwkv_dir(reverse, p_main, p_lora, vecs, mu.reshape(1, 4 * RW_LORA),
                        jnp.concatenate(parts_w).astype(BF16), jnp.concatenate(parts_a).astype(BF16),
                        out, batch, seq_all, ctx_len, *cols, gw)
    return out


def _merge_kernel(tm, nb, ctx_len, y0_ref, y1_ref, y2_ref, y3_ref, w_ref, h_ref, gb_ref, gc_ref, o_ref):
    gw = y0_ref.shape[1]
    acc = _dot(y0_ref[...], w_ref[0, 0:gw, :])
    for g, y_ref in enumerate((y1_ref, y2_ref, y3_ref), start=1):
        acc += _dot(y_ref[...], w_ref[0, g * gw:(g + 1) * gw, :])
    pos = (pl.program_id(0) % nb) * tm + _iota((tm, 1), 0)
    gate = jnp.where(pos < ctx_len, gc_ref[0], gb_ref[0])
    o_ref[...] = h_ref[...] + gate * acc


def _merge(ys, w_out_all, layer, h, mod3, batch, seq_all, ctx_len, tm, tn):
    m, d = h.shape
    gw = ys[0].shape[1]
    nb = seq_all // tm
    y_spec = pl.BlockSpec((tm, gw), lambda i, j: (i, 0))
    gate_col = 2 * (d // tn)
    return pl.pallas_call(
        functools.partial(_merge_kernel, tm, nb, ctx_len),
        grid=(m // tm, d // tn),
        in_specs=[y_spec, y_spec, y_spec, y_spec,
                  pl.BlockSpec((1, w_out_all.shape[1], tn), lambda i, j: (layer, 0, j)),
                  pl.BlockSpec((tm, tn), lambda i, j: (i, j)),
                  pl.BlockSpec((1, 1, tn), lambda i, j: (i // nb, 0, gate_col + j)),
                  pl.BlockSpec((1, 1, tn), lambda i, j: (batch, 0, gate_col + j))],
        out_specs=pl.BlockSpec((tm, tn), lambda i, j: (i, j)),
        out_shape=jax.ShapeDtypeStruct((m, d), F32),
        compiler_params=_params("parallel", "parallel"),
        name="merge",
    )(*ys, w_out_all, h, mod3, mod3)


def kernel(x, c, ctx, c_ctx, w_mod, b_mod, w_in, w_out, pool_w, pool_scale, na_q_gain, na_k_gain, na_rpb, rw_mu_rkv, rw_mu_lora, rw_w0, rw_w_up, rw_a0, rw_a_up, rw_k_k, rw_k_a, rw_r_k, rw_ln_w, rw_ln_b, sg_ln_w, sg_ln_b, sg_w, sg_b):
    batch, seq, d = x.shape
    ctx_len = ctx.shape[1]
    seq_all = ctx_len + seq
    m = batch * seq_all
    depth = w_mod.shape[0]
    gw = d // 4
    assert batch < SUBLANES and gw % LANES == 0

    h = jnp.concatenate([ctx, x], axis=1).reshape(m, d)
    cond = jnp.zeros((SUBLANES, d), F32).at[:batch].set(c).at[batch].set(c_ctx)
    mod = _modulation(cond, w_mod, b_mod)
    cos, sin = _rope_tables(seq, ctx_len)
    tm = _largest_tile(seq_all, 2 * LANES, 1024)

    lora_lo = 10 * gw
    lora_w = 4 * RW_LORA
    w_in_b = w_in.astype(BF16)
    w_out_b = w_out.astype(BF16)
    (c_pool_v, c_pool_z, c_na_q, c_na_k, c_na_v, c_na_z, c_rw_r, c_rw_k, c_rw_v, c_rw_z,
     c_sg_u, c_sg_v, c_sg_z) = range(13)

    for l in range(depth):
        mod3 = mod[l].reshape(SUBLANES, 1, 3 * d)
        xn = _norm_mod(h, mod3, batch, seq_all, ctx_len)
        p_main = _in_proj(xn, w_in_b, l, tm, gw, w_in.shape[2] - lora_w,
                          lambda j: jnp.where(j < lora_lo // gw, j * gw, j * gw + lora_w))
        p_lora = _in_proj(xn, w_in_b, l, tm, lora_w, lora_w, lambda j: lora_lo)

        y_pool = _pool(p_main, pool_w[l].astype(BF16), pool_scale[l], seq_all, ctx_len, c_pool_v, c_pool_z)
        qn, kn, vb = _na_prep(p_main, cos, sin, na_q_gain[l], na_k_gain[l], seq_all, ctx_len,
                              c_na_q, c_na_k, c_na_v, gw)
        y_na = _na(qn, kn, vb, p_main, _na_bias(na_rpb[l]), batch, seq, ctx_len, c_na_z)
        rw = (rw_mu_rkv[l], rw_mu_lora[l], rw_w0[l], rw_w_up[l], rw_a0[l], rw_a_up[l],
              rw_k_k[l], rw_k_a[l], rw_r_k[l], rw_ln_w[l], rw_ln_b[l])
        y_rw = _rwkv(p_main, p_lora, rw, batch, seq_all, ctx_len, (c_rw_r, c_rw_k, c_rw_v, c_rw_z), gw)
        y_sg = _sgu(p_main, sg_ln_w[l], sg_ln_b[l], sg_w[l], sg_b[l], c_sg_u, c_sg_v, c_sg_z)
        h = _merge((y_pool, y_na, y_rw, y_sg), w_out_b, l, h, mod3, batch, seq_all, ctx_len, tm, gw)

    return h.reshape(batch, seq_all, d)[:, ctx_len:, :]
```

```python
import functools

import jax
import jax.numpy as jnp
from jax import lax
from jax.experimental import pallas as pl
from jax.experimental.pallas import tpu as pltpu

F32 = jnp.float32
BF16 = jnp.bfloat16

LANES = 128
SUBLANES = 8
HEAD = 64
GRID_W = 64
POOL_WINDOWS = (2, 4, 8, 16)
NA_KH = 8
NA_KW = 16
NA_TILE_GROUP = 8
ROPE_BASE = 10000.0
RW_LORA = 64
RW_CHUNK = 64
RW_BLOCK_CHUNKS = 2
GN_EPS = 64e-5
SGU_CHUNK = 128
SGU_GROUPS = 8
LN_EPS = 1e-5
NORM_EPS = 1e-6
MASKED = -1e30
VMEM_LIMIT = 56 * 1024 * 1024


def _dot(a, b):
    return jnp.dot(a, b, preferred_element_type=F32)


def _dot_nt(a, b):
    return lax.dot_general(a, b, (((1,), (1,)), ((), ())), preferred_element_type=F32)


def _dot_tn(a, b):
    return lax.dot_general(a, b, (((0,), (0,)), ((), ())), preferred_element_type=F32)


def _silu(x):
    return x * jax.nn.sigmoid(x)


def _split3(x):
    hi = x.astype(BF16)
    r = x - hi.astype(F32)
    mid = r.astype(BF16)
    lo = (r - mid.astype(F32)).astype(BF16)
    return hi, mid, lo


def _iota(shape, dim):
    return lax.broadcasted_iota(jnp.int32, shape, dim)


def _params(*sem):
    return pltpu.CompilerParams(dimension_semantics=sem, vmem_limit_bytes=VMEM_LIMIT)


def _largest_tile(n, unit, cap):
    best = unit
    for t in range(unit, cap + 1, unit):
        if n % t == 0:
            best = t
    return best


def _mod_kernel(c_ref, w_ref, b_ref, o_ref):
    cs = _silu(c_ref[...]).astype(BF16)
    o_ref[0] = _dot(cs, w_ref[0].astype(BF16)) + b_ref[0]


def _modulation(cond, w_mod, b_mod):
    depth, d, n = w_mod.shape
    tn = 512
    return pl.pallas_call(
        _mod_kernel,
        grid=(depth, n // tn),
        in_specs=[
            pl.BlockSpec((SUBLANES, d), lambda l, j: (0, 0)),
            pl.BlockSpec((1, d, tn), lambda l, j: (l, 0, j)),
            pl.BlockSpec((1, 1, tn), lambda l, j: (l, 0, j)),
        ],
        out_specs=pl.BlockSpec((1, SUBLANES, tn), lambda l, j: (l, 0, j)),
        out_shape=jax.ShapeDtypeStruct((depth, SUBLANES, n), F32),
        compiler_params=_params("parallel", "parallel"),
        name="modulation",
    )(cond, w_mod, b_mod.reshape(depth, 1, n))


def _norm_kernel(h_ref, shift_ref, scale_ref, o_ref):
    x = h_ref[...]
    xn = x * lax.rsqrt(jnp.mean(x * x, axis=-1, keepdims=True) + NORM_EPS)
    o_ref[...] = (xn * (1.0 + scale_ref[0]) + shift_ref[0]).astype(BF16)


def _norm_mod(h, mod3, batch, seq_all, ctx_len):
    m, d = h.shape
    tb = _largest_tile(ctx_len, SUBLANES, 256)
    nb = seq_all // tb

    def row(i):
        return jnp.where(i % nb < ctx_len // tb, batch, i // nb)

    return pl.pallas_call(
        _norm_kernel,
        grid=(m // tb,),
        in_specs=[
            pl.BlockSpec((tb, d), lambda i: (i, 0)),
            pl.BlockSpec((1, 1, d), lambda i: (row(i), 0, 0)),
            pl.BlockSpec((1, 1, d), lambda i: (row(i), 0, 1)),
        ],
        out_specs=pl.BlockSpec((tb, d), lambda i: (i, 0)),
        out_shape=jax.ShapeDtypeStruct((m, d), BF16),
        compiler_params=_params("parallel"),
        name="norm_mod",
    )(h, mod3, mod3)


def _mm_kernel(a_ref, w_ref, o_ref):
    o_ref[...] = _dot(a_ref[...], w_ref[0])


def _in_proj(a, w_all, layer, tm, tn, n_out, col_start):
    m, k = a.shape
    w_spec = pl.BlockSpec((pl.Element(1), pl.Element(k), pl.Element(tn)),
                          lambda i, j: (layer, 0, pl.multiple_of(col_start(j), LANES)))
    return pl.pallas_call(
        _mm_kernel,
        grid=(m // tm, n_out // tn),
        in_specs=[pl.BlockSpec((tm, k), lambda i, j: (i, 0)), w_spec],
        out_specs=pl.BlockSpec((tm, tn), lambda i, j: (i, j)),
        out_shape=jax.ShapeDtypeStruct((m, n_out), F32),
        compiler_params=_params("parallel", "parallel"),
        name="in_proj",
    )(a, w_all)


def _pool_kernel(tb, nb, ctx_len, seq_all, v_ref, vp_ref, vn_ref, z_ref, pw_ref, ps_ref, o_ref):
    s0 = (pl.program_id(0) % nb) * tb
    is_ctx = s0 < ctx_len
    seq_lo = jnp.where(is_ctx, 0, ctx_len) - s0
    seq_hi = jnp.where(is_ctx, ctx_len, seq_all) - s0
    v = v_ref[...]
    ext = jnp.concatenate([vp_ref[...], v, vn_ref[...]], axis=0)
    halo = SUBLANES
    t = _iota((tb, tb + 2 * halo), 0)
    e = _iota((tb, tb + 2 * halo), 1) - halo
    t1 = _iota((tb, 1), 0)
    pool_ch = v.shape[1] // len(POOL_WINDOWS)
    outs = []
    for g, win in enumerate(POOL_WINDOWS):
        lo = jnp.maximum(t - win // 2, seq_lo)
        hi = jnp.minimum(t - win // 2 + win, seq_hi)
        band = jnp.where((e >= lo) & (e < hi), 1.0, 0.0).astype(BF16)
        cnt = (jnp.minimum(t1 - win // 2 + win, seq_hi) - jnp.maximum(t1 - win // 2, seq_lo)).astype(F32)
        ch = slice(g * pool_ch, (g + 1) * pool_ch)
        p_hi, p_mid, p_lo = _split3(ext[:, ch])
        mean = (_dot(band, p_hi) + _dot(band, p_mid) + _dot(band, p_lo)) / cnt
        outs.append(_dot((mean - v[:, ch]).astype(BF16), pw_ref[g]))
    y = jnp.concatenate(outs, axis=1) * ps_ref[...]
    o_ref[...] = (y * _silu(z_ref[...])).astype(BF16)


def _sgu_kernel(u_ref, v_ref, z_ref, lnw_ref, lnb_ref, ws_ref, bias_ref, o_ref):
    ch = v_ref.shape[1] // SGU_GROUPS
    for c0 in range(0, v_ref.shape[0], SGU_CHUNK):
        rows = slice(c0, c0 + SGU_CHUNK)
        v = v_ref[rows, :]
        d = v - jnp.mean(v, axis=-1, keepdims=True)
        var = jnp.mean(d * d, axis=-1, keepdims=True)
        vn = (d * lax.rsqrt(var + LN_EPS) * lnw_ref[...] + lnb_ref[...]).astype(BF16)
        mixed = jnp.concatenate(
            [_dot(ws_ref[g], vn[:, g * ch:(g + 1) * ch]) for g in range(SGU_GROUPS)], axis=1)
        o_ref[rows, :] = (u_ref[rows, :] * (mixed + bias_ref[...]) * _silu(z_ref[rows, :])).astype(BF16)


def _na_prep_kernel(q_ref, k_ref, v_ref, cos_ref, sin_ref, qg_ref, kg_ref, ones_ref, qo_ref, ko_ref, vo_ref):
    cos = cos_ref[...]
    sin = sin_ref[...]
    ones = ones_ref[...]
    quarter = HEAD // 4
    low_quarter = (_iota((1, LANES), 1) % (2 * quarter)) < quarter

    def prep(x_ref, gain, scale, o_ref):
        for j in range(x_ref.shape[1] // LANES):
            sl = slice(j * LANES, (j + 1) * LANES)
            x = x_ref[:, sl]
            sq = x * x
            sq_hi = sq.astype(BF16)
            sq_lo = (sq - sq_hi.astype(F32)).astype(BF16)
            ms = (_dot(sq_hi, ones) + _dot(sq_lo, ones)) * (1.0 / HEAD)
            xn = x * lax.rsqrt(ms + NORM_EPS) * gain
            partner = jnp.where(low_quarter, pltpu.roll(xn, LANES - quarter, 1), pltpu.roll(xn, quarter, 1))
            o_ref[:, sl] = ((xn * cos + partner * sin) * scale).astype(BF16)

    prep(q_ref, qg_ref[...], HEAD ** -0.5, qo_ref)
    prep(k_ref, kg_ref[...], 1.0, ko_ref)
    vo_ref[...] = v_ref[...].astype(BF16)


def _rope_tables(seq, ctx_len):
    t = jnp.arange(seq)
    n_freq = HEAD // 4
    inv = ROPE_BASE ** (-jnp.arange(n_freq, dtype=F32) / n_freq)
    ang_r = (t // GRID_W).astype(F32)[:, None] * inv
    ang_c = (t % GRID_W).astype(F32)[:, None] * inv
    cos = jnp.concatenate([jnp.cos(ang_r), jnp.cos(ang_r), jnp.cos(ang_c), jnp.cos(ang_c)], axis=1)
    sin = jnp.concatenate([-jnp.sin(ang_r), jnp.sin(ang_r), -jnp.sin(ang_c), jnp.sin(ang_c)], axis=1)
    cos = jnp.concatenate([jnp.ones((ctx_len, HEAD), F32), cos], axis=0)
    sin = jnp.concatenate([jnp.zeros((ctx_len, HEAD), F32), sin], axis=0)
    reps = LANES // HEAD
    return jnp.tile(cos, (1, reps)), jnp.tile(sin, (1, reps))


_N_POOL_IN, _N_SGU_IN, _N_PREP_IN = 6, 7, 8


def _local_kernel(tb, nb, ctx_len, seq_all, *refs):
    ins, outs = refs[:-5], refs[-5:]
    pool_in = ins[:_N_POOL_IN]
    sgu_in = ins[_N_POOL_IN:_N_POOL_IN + _N_SGU_IN]
    prep_in = ins[_N_POOL_IN + _N_SGU_IN:]
    _pool_kernel(tb, nb, ctx_len, seq_all, *pool_in, outs[0])
    _sgu_kernel(*sgu_in, outs[1])
    _na_prep_kernel(*prep_in, *outs[2:])


def _local_mixers(p_main, cols, pool_w, pool_scale, ln_w, ln_b, w_s, b_s, cos, sin, q_gain, k_gain,
                  seq_all, ctx_len, gw):
    m = p_main.shape[0]
    tb = _largest_tile(ctx_len, SGU_CHUNK, 2 * SGU_CHUNK)
    assert ctx_len % tb == 0 and seq_all % tb == 0
    nb = seq_all // tb
    hb = tb // SUBLANES
    last = m // SUBLANES - 1
    reps = LANES // HEAD
    lane_head = jnp.arange(LANES) // HEAD
    ones = (lane_head[:, None] == lane_head[None, :]).astype(BF16)
    bias = jnp.repeat(b_s.T, gw // SGU_GROUPS, axis=1)
    tok = lambda col: pl.BlockSpec((tb, gw), lambda i: (i, col))
    vec = lambda w: pl.BlockSpec((1, w), lambda i: (0, 0))
    full = lambda arr: pl.BlockSpec(arr.shape, lambda i: (0,) * arr.ndim)
    tab = pl.BlockSpec((tb, LANES), lambda i: (i % nb, 0))
    pool_specs = [tok(cols["pool_v"]),
                  pl.BlockSpec((SUBLANES, gw), lambda i: (jnp.maximum(i * hb - 1, 0), cols["pool_v"])),
                  pl.BlockSpec((SUBLANES, gw), lambda i: (jnp.minimum((i + 1) * hb, last), cols["pool_v"])),
                  tok(cols["pool_z"]), full(pool_w), vec(gw)]
    pool_args = [p_main, p_main, p_main, p_main, pool_w, pool_scale.reshape(1, gw)]
    w_s = w_s.astype(BF16)
    sgu_specs = [tok(cols["sg_u"]), tok(cols["sg_v"]), tok(cols["sg_z"]), vec(gw), vec(gw), full(w_s), full(bias)]
    sgu_args = [p_main, p_main, p_main, ln_w.reshape(1, gw), ln_b.reshape(1, gw), w_s, bias]
    prep_specs = [tok(cols["na_q"]), tok(cols["na_k"]), tok(cols["na_v"]), tab, tab, vec(LANES), vec(LANES), full(ones)]
    prep_args = [p_main, p_main, p_main, cos, sin, jnp.tile(q_gain, reps).reshape(1, LANES),
                 jnp.tile(k_gain, reps).reshape(1, LANES), ones]
    assert (len(pool_specs), len(sgu_specs), len(prep_specs)) == (_N_POOL_IN, _N_SGU_IN, _N_PREP_IN)
    out = jax.ShapeDtypeStruct((m, gw), BF16)
    return pl.pallas_call(
        functools.partial(_local_kernel, tb, nb, ctx_len, seq_all),
        grid=(m // tb,),
        in_specs=pool_specs + sgu_specs + prep_specs,
        out_specs=[pl.BlockSpec((tb, gw), lambda i: (i, 0))] * 5,
        out_shape=[out] * 5,
        compiler_params=_params("parallel"),
        name="local_mixers",
    )(*pool_args, *sgu_args, *prep_args)


def _na_bias(rpb):
    d = jnp.arange(NA_KH)[:, None]
    i = jnp.arange(NA_KH)[None, :]
    rel_r = i - d + NA_KH - 1
    qc = jnp.arange(GRID_W)[:, None]
    kc = jnp.arange(GRID_W)[None, :]
    cs = jnp.clip(qc - NA_KW // 2, 0, GRID_W - NA_KW)
    valid = (kc >= cs) & (kc < cs + NA_KW)
    rel_c = jnp.clip(kc - qc + NA_KW - 1, 0, 2 * NA_KW - 2)
    t = rpb[:, rel_r][:, :, :, rel_c]
    t = jnp.where(valid, t, MASKED)
    heads = rpb.shape[0]
    reps = LANES // HEAD
    t = t.transpose(1, 0, 3, 2, 4).reshape(NA_KH, heads // reps, reps * GRID_W, NA_KH * GRID_W)
    return jnp.concatenate([t, jnp.full((1,) + t.shape[1:], MASKED, F32)], axis=0)


def _na_kernel(q_ref, kw_ref, vw_ref, kc_ref, vc_ref, bias_ref, z_ref, o_ref):
    head0 = _iota((1, LANES), 1) < HEAD
    rows = q_ref.shape[0]
    n_tiles = q_ref.shape[1] // LANES
    for g0 in range(0, n_tiles, NA_TILE_GROUP):
        ids = range(g0, min(g0 + NA_TILE_GROUP, n_tiles))
        tiles = {j: slice(j * LANES, (j + 1) * LANES) for j in ids}
        qs = {}
        for j in ids:
            q = q_ref[:, tiles[j]]
            zero = jnp.zeros_like(q)
            qs[j] = jnp.concatenate([jnp.where(head0, q, zero), jnp.where(head0, zero, q)], axis=0)
        s_w = {j: _dot_nt(qs[j], kw_ref[:, tiles[j]]) + bias_ref[0, j] for j in ids}
        s_c = {j: _dot_nt(qs[j], kc_ref[:, tiles[j]]) for j in ids}
        mx = {j: jnp.maximum(jnp.max(s_w[j], axis=-1, keepdims=True), jnp.max(s_c[j], axis=-1, keepdims=True))
              for j in ids}
        p_w = {j: jnp.exp(s_w[j] - mx[j]) for j in ids}
        p_c = {j: jnp.exp(s_c[j] - mx[j]) for j in ids}
        den = {j: jnp.sum(p_w[j], axis=-1, keepdims=True) + jnp.sum(p_c[j], axis=-1, keepdims=True) for j in ids}
        o = {j: _dot(p_w[j].astype(BF16), vw_ref[:, tiles[j]]) + _dot(p_c[j].astype(BF16), vc_ref[:, tiles[j]])
             for j in ids}
        for j in ids:
            oj = o[j] / den[j]
            y = jnp.where(head0, oj[:rows], oj[rows:])
            o_ref[:, tiles[j]] = (y * _silu(z_ref[:, tiles[j]])).astype(BF16)


def _na(qn, kn, vb, p_main, bias, batch, seq, ctx_len, col_z):
    m, gw = qn.shape
    seq_all = seq + ctx_len
    rows = seq // GRID_W
    assert rows >= NA_KH and ctx_len % GRID_W == 0 and seq_all % ctx_len == 0
    win = NA_KH * GRID_W

    def q_blk(b, r):
        return jnp.where(r < rows, (b * seq_all + ctx_len) // GRID_W + r, (b * seq_all) // GRID_W + r - rows)

    def win_start(b, r):
        start = b * seq_all + ctx_len + jnp.clip(r - NA_KH // 2, 0, rows - NA_KH) * GRID_W
        return pl.multiple_of(start, GRID_W)

    def bias_blk(b, r):
        return jnp.where(r < rows, r - jnp.clip(r - NA_KH // 2, 0, rows - NA_KH), NA_KH)

    window = pl.BlockSpec((pl.Element(win), pl.Element(gw)), lambda b, r: (win_start(b, r), 0))
    ctx_kv = pl.BlockSpec((ctx_len, gw), lambda b, r: (b * (seq_all // ctx_len), 0))
    return pl.pallas_call(
        _na_kernel,
        grid=(batch, rows + ctx_len // GRID_W),
        in_specs=[
            pl.BlockSpec((GRID_W, gw), lambda b, r: (q_blk(b, r), 0)),
            window, window, ctx_kv, ctx_kv,
            pl.BlockSpec((1,) + bias.shape[1:], lambda b, r: (bias_blk(b, r), 0, 0, 0)),
            pl.BlockSpec((GRID_W, gw), lambda b, r: (q_blk(b, r), col_z)),
        ],
        out_specs=pl.BlockSpec((GRID_W, gw), lambda b, r: (q_blk(b, r), 0)),
        out_shape=jax.ShapeDtypeStruct((m, gw), BF16),
        compiler_params=_params("parallel", "arbitrary"),
        name="na_attention",
    )(qn, kn, vb, kn, vb, bias, p_main)


(_MU_R, _MU_K, _MU_V, _W0, _A0, _K_K, _K_A, _R_K, _LN_W, _LN_B) = range(10)


def _rwkv_kernel(reverse, chunk, n_ctx_blocks, r_ref, k_ref, v_ref, lo_ref, rh_ref, kh_ref, vh_ref, loh_ref,
                 vec_ref, mul_ref, wup_ref, aup_ref, *rest):
    if reverse:
        fwd_ref, z_ref, o_ref, state_ref = rest
    else:
        o_ref, state_ref = rest
    step = pl.program_id(1)
    rows = r_ref.shape[0]
    subs = rows // chunk

    @pl.when(step == 0)
    def _():
        state_ref[...] = jnp.zeros_like(state_ref)

    seq_start = jnp.logical_or(step == 0, step == n_ctx_blocks)
    row_id = _iota((rows, 1), 0)

    def lerp_shift(x_ref, halo_ref, mu):
        x = x_ref[...]
        if reverse:
            edge, rolled, edge_row = halo_ref[0:1, :], pltpu.roll(x, rows - 1, 0), rows - 1
        else:
            edge, rolled, edge_row = halo_ref[SUBLANES - 1:SUBLANES, :], pltpu.roll(x, 1, 0), 0
        edge = jnp.where(seq_start, jnp.zeros_like(edge), edge)
        return x + (jnp.where(row_id == edge_row, edge, rolled) - x) * mu

    vec = lambda n: vec_ref[n:n + 1, :]
    rl = lerp_shift(r_ref, rh_ref, vec(_MU_R))
    kl = lerp_shift(k_ref, kh_ref, vec(_MU_K))
    vl = lerp_shift(v_ref, vh_ref, vec(_MU_V))
    lol = lerp_shift(lo_ref, loh_ref, mul_ref[...])
    neg_w_pre = -(vec(_W0) + _dot(jnp.tanh(lol).astype(BF16), wup_ref[...]))
    w = -(jnp.maximum(neg_w_pre, 0.0) + jnp.log1p(jnp.exp(-jnp.abs(neg_w_pre)))) - 0.5
    ew = jnp.exp(w)
    a = jax.nn.sigmoid(vec(_A0) + _dot(lol.astype(BF16), aup_ref[...]))

    tt = _iota((rows, rows), 0)
    ss = _iota((rows, rows), 1)
    same_chunk = (tt ^ ss) < chunk
    tri = jnp.where(same_chunk & ((ss >= tt) if reverse else (ss <= tt)), 1.0, 0.0).astype(BF16)
    e_hi, e_mid, e_lo = _split3(ew)
    cs = _dot(tri, e_hi) + _dot(tri, e_mid) + _dot(tri, e_lo)
    kk_raw = kl * vec(_K_K)
    kt = kl * (1.0 + (a - 1.0) * vec(_K_A))
    w_incl = jnp.exp(-cs)
    w_excl = jnp.exp(ew - cs)
    w_inv = jnp.exp(cs)

    head0 = _iota((1, LANES), 1) < HEAD
    t_loc = _iota((chunk, LANES), 0)
    s_loc = _iota((chunk, LANES), 1) & (chunk - 1)
    earlier = (s_loc > t_loc) if reverse else (s_loc < t_loc)
    earlier_eq = (s_loc >= t_loc) if reverse else (s_loc <= t_loc)
    levels = chunk.bit_length() - 1
    later_half = 0 if reverse else 1
    sibling = [(((t_loc >> lb) ^ (s_loc >> lb)) == 1) & (((t_loc >> lb) & 1) == later_half)
               for lb in range(levels)]
    eye = jnp.where(s_loc == t_loc, 1.0, 0.0)
    lane_r = _iota((LANES, LANES), 0)
    lane_c = _iota((LANES, LANES), 1)
    lane_eye = lane_r == lane_c
    same_head = (lane_r ^ lane_c) < HEAD

    def stack(x):
        z = jnp.zeros_like(x)
        return jnp.concatenate([jnp.where(head0, x, z), jnp.where(head0, z, x)], axis=0)

    def head_sum(x):
        s0 = jnp.sum(jnp.where(head0, x, 0.0), axis=-1, keepdims=True)
        s1 = jnp.sum(jnp.where(head0, 0.0, x), axis=-1, keepdims=True)
        return jnp.where(head0, s0, s1)

    tiles = [slice(j * LANES, (j + 1) * LANES) for j in range(r_ref.shape[1] // LANES)]
    units = [(slice(u * chunk, (u + 1) * chunk), sl) for u in range(subs) for sl in tiles]
    kap, bhkh, rh32, rh, kw, bw, vb, vs, w_tot = [], [], [], [], [], [], [], [], []
    for ru, sl in units:
        kk = kk_raw[ru, sl]
        kk = kk * lax.rsqrt(jnp.maximum(head_sum(kk * kk), 1e-12))
        beta = kk * a[ru, sl]
        cs_u = cs[ru, sl]
        tot = cs_u[0:1, :] if reverse else cs_u[chunk - 1:chunk, :]
        w_rem = jnp.exp(cs_u - tot)
        w_tot.append(jnp.exp(-tot))
        kap.append((kk * w_excl[ru, sl]).astype(BF16))
        bhkh.append(jnp.concatenate([stack(beta * w_inv[ru, sl]), stack(kt[ru, sl] * w_inv[ru, sl])],
                                    axis=0).astype(BF16))
        rh32.append(rl[ru, sl] * w_incl[ru, sl])
        rh.append(rh32[-1].astype(BF16))
        kw.append((kt[ru, sl] * w_rem).astype(BF16))
        bw.append((beta * w_rem).astype(BF16))
        vb.append(vl[ru, sl].astype(BF16))
        vs.append(stack(vb[-1]))
    n = range(len(units))

    a_all = [_dot_nt(jnp.concatenate([kap[j], rh[j]], axis=0), bhkh[j]) for j in n]
    a_kb = [a_all[j][:chunk, :LANES] for j in n]
    a_rb = [jnp.where(earlier_eq, a_all[j][chunk:, :LANES], 0.0).astype(BF16) for j in n]
    a_xk = [jnp.concatenate([jnp.where(earlier, a_all[j][:chunk, LANES:], 0.0),
                             jnp.where(earlier_eq, a_all[j][chunk:, LANES:], 0.0)], axis=0).astype(BF16) for j in n]
    xk_v = [_dot(a_xk[j], vs[j]) for j in n]
    rhs = [xk_v[j][:chunk].astype(BF16) for j in n]

    inv = [eye - jnp.where(sibling[0], a_kb[j], 0.0) for j in n]
    for lb in range(1, levels):
        inv_b = [inv[j].astype(BF16) for j in n]
        half = [_dot(inv_b[j], stack(jnp.where(sibling[lb], a_kb[j], 0.0).astype(BF16))).astype(BF16) for j in n]
        inv = [inv[j] - _dot(half[j], stack(inv_b[j])) for j in n]
    inv_b = [inv[j].astype(BF16) for j in n]

    pq = [_dot(inv_b[j], jnp.concatenate([stack(kap[j]), stack(rhs[j])], axis=1)) for j in n]
    p_mat = [pq[j][:, :LANES].astype(BF16) for j in n]
    q_mat = [pq[j][:, LANES:].astype(BF16) for j in n]
    rb_pq = [_dot(a_rb[j], jnp.concatenate([stack(p_mat[j]), stack(q_mat[j])], axis=1)) for j in n]
    r_eff = [(rh32[j] - rb_pq[j][:, :LANES]).astype(BF16) for j in n]
    y0 = [xk_v[j][chunk:] - rb_pq[j][:, LANES:] for j in n]
    zero_b = jnp.zeros((chunk, LANES), BF16)
    upd = [_dot_tn(jnp.concatenate([kw[j], bw[j]], axis=0),
                   jnp.concatenate([jnp.concatenate([vb[j], zero_b], axis=1),
                                    jnp.concatenate([-q_mat[j], p_mat[j]], axis=1)], axis=0)) for j in n]
    h_add = [jnp.where(same_head, upd[j][:, :LANES], 0.0) for j in n]
    bp = [jnp.where(same_head, upd[j][:, LANES:], 0.0).astype(BF16) for j in n]

    w_col = [jnp.sum(jnp.where(lane_eye, w_tot[i], 0.0), axis=1, keepdims=True) for i in n]
    state = [state_ref[j] for j in range(len(tiles))]
    ys = [None] * len(units)
    for u in (range(subs - 1, -1, -1) if reverse else range(subs)):
        for j in range(len(tiles)):
            i = u * len(tiles) + j
            h_b = state[j].astype(BF16)
            ys[i] = _dot(r_eff[i], h_b) + y0[i]
            state[j] = state[j] * w_col[i] - _dot(bp[i], h_b) + h_add[i]
    for j in range(len(tiles)):
        state_ref[j] = state[j]

    for i, (ru, sl) in enumerate(units):
        y = ys[i]
        d = y - head_sum(y) * (1.0 / HEAD)
        var = head_sum(d * d) * (1.0 / HEAD)
        out = d * lax.rsqrt(var + GN_EPS) * vec(_LN_W)[:, sl] + vec(_LN_B)[:, sl]
        out = out + head_sum(rl[ru, sl] * kt[ru, sl] * vec(_R_K)[:, sl]) * vl[ru, sl]
        if reverse:
            o_ref[ru, sl] = ((fwd_ref[ru, sl] + out) * _silu(z_ref[ru, sl])).astype(BF16)
        else:
            o_ref[ru, sl] = out


def _rwkv_dir(reverse, p_main, p_lora, vecs, mu_lora, w_up, a_up, fwd_out, batch, seq_all, ctx_len,
              col_r, col_k, col_v, col_z, gw):
    m = p_main.shape[0]
    chunk = RW_CHUNK
    assert 2 * chunk == LANES and chunk == HEAD
    rows = RW_BLOCK_CHUNKS * chunk
    assert ctx_len % rows == 0 and seq_all % rows == 0
    n_blocks = seq_all // rows
    n_ctx = ctx_len // rows
    per8 = rows // SUBLANES
    last8 = m // SUBLANES - 1
    nl = p_lora.shape[1]

    def block_of(i):
        if reverse:
            return jnp.where(i < n_ctx, n_ctx - 1 - i, n_blocks - 1 + n_ctx - i)
        return i

    def blk(b, i):
        return b * n_blocks + block_of(i)

    def halo(b, i):
        if reverse:
            return jnp.minimum((blk(b, i) + 1) * per8, last8)
        return jnp.maximum(blk(b, i) * per8 - 1, 0)

    tok = lambda col: pl.BlockSpec((rows, gw), lambda b, i: (blk(b, i), col))
    hal = lambda col: pl.BlockSpec((SUBLANES, gw), lambda b, i: (halo(b, i), col))
    const = lambda arr: pl.BlockSpec(arr.shape, lambda b, i: (0,) * arr.ndim)
    in_specs = [tok(col_r), tok(col_k), tok(col_v), pl.BlockSpec((rows, nl), lambda b, i: (blk(b, i), 0)),
                hal(col_r), hal(col_k), hal(col_v), pl.BlockSpec((SUBLANES, nl), lambda b, i: (halo(b, i), 0)),
                const(vecs), const(mu_lora), const(w_up), const(a_up)]
    args = [p_main, p_main, p_main, p_lora, p_main, p_main, p_main, p_lora, vecs, mu_lora, w_up, a_up]
    if reverse:
        in_specs += [pl.BlockSpec((rows, gw), lambda b, i: (blk(b, i), 0)), tok(col_z)]
        args += [fwd_out, p_main]
    return pl.pallas_call(
        functools.partial(_rwkv_kernel, reverse, chunk, n_ctx),
        grid=(batch, n_blocks),
        in_specs=in_specs,
        out_specs=pl.BlockSpec((rows, gw), lambda b, i: (blk(b, i), 0)),
        out_shape=jax.ShapeDtypeStruct((m, gw), BF16 if reverse else F32),
        scratch_shapes=[pltpu.VMEM((gw // LANES, LANES, LANES), F32)],
        compiler_params=_params("parallel", "arbitrary"),
        name="rwkv_rev" if reverse else "rwkv_fwd",
    )(*args)


def _rwkv(p_main, p_lora, rw, batch, seq_all, ctx_len, cols, gw):
    mu_rkv, mu_lora, w0, w_up, a0, a_up, k_k, k_a, r_k, ln_w, ln_b = rw
    out = None
    for d, reverse in enumerate((False, True)):
        rows = [mu_rkv[d, 0], mu_rkv[d, 1], mu_rkv[d, 2], w0[d], a0[d], k_k, k_a, r_k.reshape(-1), ln_w, ln_b]
        vecs = jnp.zeros((2 * SUBLANES, gw), F32).at[:len(rows)].set(jnp.stack(rows))
        zeros = jnp.zeros((RW_LORA, gw), F32)
        parts_w = [zeros] * 4
        parts_a = [zeros] * 4
        parts_w[d] = w_up[d]
        parts_a[2 + d] = a_up[d]
        mu = jnp.zeros((4, RW_LORA), F32).at[d].set(mu_lora[d, 0]).at[2 + d].set(mu_lora[d, 1])
        out = _rwkv_dir(reverse, p_main, p_lora, vecs, mu.reshape(1, 4 * RW_LORA),
                        jnp.concatenate(parts_w).astype(BF16), jnp.concatenate(parts_a).astype(BF16),
                        out, batch, seq_all, ctx_len, *cols, gw)
    return out


def _merge_kernel(tm, nb, first_pos, ctx_len, y0_ref, y1_ref, y2_ref, y3_ref, w_ref, h_ref, gb_ref, gc_ref, o_ref):
    gw = y0_ref.shape[1]
    acc = _dot(y0_ref[...], w_ref[0, 0:gw, :])
    for g, y_ref in enumerate((y1_ref, y2_ref, y3_ref), start=1):
        acc += _dot(y_ref[...], w_ref[0, g * gw:(g + 1) * gw, :])
    pos = first_pos + (pl.program_id(0) % nb) * tm + _iota((tm, 1), 0)
    gate = jnp.where(pos < ctx_len, gc_ref[0], gb_ref[0])
    o_ref[...] = h_ref[...] + gate * acc


def _merge(ys, w_out_all, layer, h, mod3, batch, seq_all, ctx_len, tn, latent_only):
    d = h.shape[1]
    gw = ys[0].shape[1]
    first_pos = ctx_len if latent_only else 0
    rows = seq_all - first_pos
    tm = _largest_tile(rows, 2 * LANES, 1024)
    nb = rows // tm

    def row_start(i):
        return pl.multiple_of((i // nb) * seq_all + first_pos + (i % nb) * tm, 2 * LANES)

    y_spec = pl.BlockSpec((pl.Element(tm), pl.Element(gw)), lambda i, j: (row_start(i), 0))
    gate_col = 2 * (d // tn)
    return pl.pallas_call(
        functools.partial(_merge_kernel, tm, nb, first_pos, ctx_len),
        grid=(batch * nb, d // tn),
        in_specs=[y_spec, y_spec, y_spec, y_spec,
                  pl.BlockSpec((1, w_out_all.shape[1], tn), lambda i, j: (layer, 0, j)),
                  pl.BlockSpec((pl.Element(tm), pl.Element(tn)), lambda i, j: (row_start(i), j * tn)),
                  pl.BlockSpec((1, 1, tn), lambda i, j: (i // nb, 0, gate_col + j)),
                  pl.BlockSpec((1, 1, tn), lambda i, j: (batch, 0, gate_col + j))],
        out_specs=pl.BlockSpec((tm, tn), lambda i, j: (i, j)),
        out_shape=jax.ShapeDtypeStruct((batch * rows, d), F32),
        compiler_params=_params("parallel", "parallel"),
        name="merge",
    )(*ys, w_out_all, h, mod3, mod3)


def kernel(x, c, ctx, c_ctx, w_mod, b_mod, w_in, w_out, pool_w, pool_scale, na_q_gain, na_k_gain, na_rpb, rw_mu_rkv, rw_mu_lora, rw_w0, rw_w_up, rw_a0, rw_a_up, rw_k_k, rw_k_a, rw_r_k, rw_ln_w, rw_ln_b, sg_ln_w, sg_ln_b, sg_w, sg_b):
    batch, seq, d = x.shape
    ctx_len = ctx.shape[1]
    seq_all = ctx_len + seq
    m = batch * seq_all
    depth = w_mod.shape[0]
    gw = d // 4
    assert batch < SUBLANES and gw % LANES == 0

    h = jnp.concatenate([ctx, x], axis=1).reshape(m, d)
    cond = jnp.zeros((SUBLANES, d), F32).at[:batch].set(c).at[batch].set(c_ctx)
    mod = _modulation(cond, w_mod, b_mod)
    cos, sin = _rope_tables(seq, ctx_len)
    tm = _largest_tile(seq_all, 2 * LANES, 1024)

    lora_lo = 10 * gw
    lora_w = 4 * RW_LORA
    w_in_b = w_in.astype(BF16)
    w_out_b = w_out.astype(BF16)
    cols = {name: j for j, name in enumerate(("pool_v", "pool_z", "na_q", "na_k", "na_v", "na_z", "rw_r", "rw_k",
                                               "rw_v", "rw_z", "sg_u", "sg_v", "sg_z"))}

    for l in range(depth):
        mod3 = mod[l].reshape(SUBLANES, 1, 3 * d)
        xn = _norm_mod(h, mod3, batch, seq_all, ctx_len)
        p_main = _in_proj(xn, w_in_b, l, tm, gw, w_in.shape[2] - lora_w,
                          lambda j: jnp.where(j < lora_lo // gw, j * gw, j * gw + lora_w))
        p_lora = _in_proj(xn, w_in_b, l, tm, lora_w, lora_w, lambda j: lora_lo)

        y_pool, y_sg, qn, kn, vb = _local_mixers(
            p_main, cols, pool_w[l].astype(BF16), pool_scale[l], sg_ln_w[l], sg_ln_b[l], sg_w[l], sg_b[l],
            cos, sin, na_q_gain[l], na_k_gain[l], seq_all, ctx_len, gw)
        y_na = _na(qn, kn, vb, p_main, _na_bias(na_rpb[l]), batch, seq, ctx_len, cols["na_z"])
        rw = (rw_mu_rkv[l], rw_mu_lora[l], rw_w0[l], rw_w_up[l], rw_a0[l], rw_a_up[l],
              rw_k_k[l], rw_k_a[l], rw_r_k[l], rw_ln_w[l], rw_ln_b[l])
        y_rw = _rwkv(p_main, p_lora, rw, batch, seq_all, ctx_len, [cols[k] for k in ("rw_r", "rw_k", "rw_v", "rw_z")], gw)
        h = _merge((y_pool, y_na, y_rw, y_sg), w_out_b, l, h, mod3, batch, seq_all, ctx_len, gw,
                   latent_only=(l == depth - 1))

    return h.reshape(batch, seq, d)
```

```python
import functools

import jax
import jax.numpy as jnp
from jax import lax
from jax.experimental import pallas as pl
from jax.experimental.pallas import tpu as pltpu

F32 = jnp.float32
BF16 = jnp.bfloat16

LANES = 128
SUBLANES = 8
HEAD = 64
GRID_W = 64
POOL_WINDOWS = (2, 4, 8, 16)
NA_KH = 8
NA_KW = 16
NA_TILE_GROUP = 8
ROPE_BASE = 10000.0
RW_LORA = 64
RW_CHUNK = 64
RW_BLOCK_CHUNKS = 2
GN_EPS = 64e-5
SGU_CHUNK = 128
SGU_GROUPS = 8
LN_EPS = 1e-5
NORM_EPS = 1e-6
MASKED = -1e30
VMEM_LIMIT = 56 * 1024 * 1024
IN_PROJ_ROWS = 1408
IN_PROJ_COLS = 512


def _dot(a, b):
    return jnp.dot(a, b, preferred_element_type=F32)


def _dot_nt(a, b):
    return lax.dot_general(a, b, (((1,), (1,)), ((), ())), preferred_element_type=F32)


def _dot_tn(a, b):
    return lax.dot_general(a, b, (((0,), (0,)), ((), ())), preferred_element_type=F32)


def _silu(x):
    return x * jax.nn.sigmoid(x)


def _split3(x):
    hi = x.astype(BF16)
    r = x - hi.astype(F32)
    mid = r.astype(BF16)
    lo = (r - mid.astype(F32)).astype(BF16)
    return hi, mid, lo


def _iota(shape, dim):
    return lax.broadcasted_iota(jnp.int32, shape, dim)


def _params(*sem):
    return pltpu.CompilerParams(dimension_semantics=sem, vmem_limit_bytes=VMEM_LIMIT)


def _largest_tile(n, unit, cap):
    best = unit
    for t in range(unit, cap + 1, unit):
        if n % t == 0:
            best = t
    return best


def _mod_kernel(c_ref, w_ref, b_ref, o_ref):
    cs = _silu(c_ref[...]).astype(BF16)
    o_ref[0] = _dot(cs, w_ref[0].astype(BF16)) + b_ref[0]


def _modulation(cond, w_mod, b_mod):
    depth, d, n = w_mod.shape
    tn = 512
    return pl.pallas_call(
        _mod_kernel,
        grid=(depth, n // tn),
        in_specs=[
            pl.BlockSpec((SUBLANES, d), lambda l, j: (0, 0)),
            pl.BlockSpec((1, d, tn), lambda l, j: (l, 0, j)),
            pl.BlockSpec((1, 1, tn), lambda l, j: (l, 0, j)),
        ],
        out_specs=pl.BlockSpec((1, SUBLANES, tn), lambda l, j: (l, 0, j)),
        out_shape=jax.ShapeDtypeStruct((depth, SUBLANES, n), F32),
        compiler_params=_params("parallel", "parallel"),
        name="modulation",
    )(cond, w_mod, b_mod.reshape(depth, 1, n))


def _norm_kernel(h_ref, shift_ref, scale_ref, o_ref):
    x = h_ref[...]
    xn = x * lax.rsqrt(jnp.mean(x * x, axis=-1, keepdims=True) + NORM_EPS)
    o_ref[...] = (xn * (1.0 + scale_ref[0]) + shift_ref[0]).astype(BF16)


def _norm_mod(h, mod3, batch, seq_all, ctx_len):
    m, d = h.shape
    tb = _largest_tile(ctx_len, SUBLANES, 256)
    nb = seq_all // tb

    def row(i):
        return jnp.where(i % nb < ctx_len // tb, batch, i // nb)

    return pl.pallas_call(
        _norm_kernel,
        grid=(m // tb,),
        in_specs=[
            pl.BlockSpec((tb, d), lambda i: (i, 0)),
            pl.BlockSpec((1, 1, d), lambda i: (row(i), 0, 0)),
            pl.BlockSpec((1, 1, d), lambda i: (row(i), 0, 1)),
        ],
        out_specs=pl.BlockSpec((tb, d), lambda i: (i, 0)),
        out_shape=jax.ShapeDtypeStruct((m, d), BF16),
        compiler_params=_params("parallel"),
        name="norm_mod",
    )(h, mod3, mod3)


def _mm_kernel(a_ref, w_ref, o_ref, wb_ref):
    @pl.when(pl.program_id(1) == 0)
    def _():
        wb_ref[...] = w_ref[0].astype(BF16)

    o_ref[...] = _dot(a_ref[...], wb_ref[...])


def _in_proj(a, w_all, layer, tm, tn, n_out, col_start):
    m, k = a.shape
    w_spec = pl.BlockSpec((pl.Element(1), pl.Element(k), pl.Element(tn)),
                          lambda j, i: (layer, 0, pl.multiple_of(col_start(j), LANES)))
    return pl.pallas_call(
        _mm_kernel,
        grid=(n_out // tn, m // tm),
        in_specs=[pl.BlockSpec((tm, k), lambda j, i: (i, 0)), w_spec],
        out_specs=pl.BlockSpec((tm, tn), lambda j, i: (i, j)),
        out_shape=jax.ShapeDtypeStruct((m, n_out), F32),
        scratch_shapes=[pltpu.VMEM((k, tn), BF16)],
        compiler_params=_params("arbitrary", "arbitrary"),
        name="in_proj",
    )(a, w_all)


def _pool_kernel(tb, nb, ctx_len, seq_all, v_ref, vp_ref, vn_ref, z_ref, pw_ref, ps_ref, o_ref):
    s0 = (pl.program_id(0) % nb) * tb
    is_ctx = s0 < ctx_len
    seq_lo = jnp.where(is_ctx, 0, ctx_len) - s0
    seq_hi = jnp.where(is_ctx, ctx_len, seq_all) - s0
    v = v_ref[...]
    ext = jnp.concatenate([vp_ref[...], v, vn_ref[...]], axis=0)
    halo = SUBLANES
    t = _iota((tb, tb + 2 * halo), 0)
    e = _iota((tb, tb + 2 * halo), 1) - halo
    t1 = _iota((tb, 1), 0)
    pool_ch = v.shape[1] // len(POOL_WINDOWS)
    outs = []
    for g, win in enumerate(POOL_WINDOWS):
        lo = jnp.maximum(t - win // 2, seq_lo)
        hi = jnp.minimum(t - win // 2 + win, seq_hi)
        band = jnp.where((e >= lo) & (e < hi), 1.0, 0.0).astype(BF16)
        cnt = (jnp.minimum(t1 - win // 2 + win, seq_hi) - jnp.maximum(t1 - win // 2, seq_lo)).astype(F32)
        ch = slice(g * pool_ch, (g + 1) * pool_ch)
        p_hi, p_mid, p_lo = _split3(ext[:, ch])
        mean = (_dot(band, p_hi) + _dot(band, p_mid) + _dot(band, p_lo)) / cnt
        outs.append(_dot((mean - v[:, ch]).astype(BF16), pw_ref[g]))
    y = jnp.concatenate(outs, axis=1) * ps_ref[...]
    o_ref[...] = (y * _silu(z_ref[...])).astype(BF16)


def _sgu_kernel(u_ref, v_ref, z_ref, lnw_ref, lnb_ref, ws_ref, bias_ref, o_ref):
    ch = v_ref.shape[1] // SGU_GROUPS
    for c0 in range(0, v_ref.shape[0], SGU_CHUNK):
        rows = slice(c0, c0 + SGU_CHUNK)
        v = v_ref[rows, :]
        d = v - jnp.mean(v, axis=-1, keepdims=True)
        var = jnp.mean(d * d, axis=-1, keepdims=True)
        vn = (d * lax.rsqrt(var + LN_EPS) * lnw_ref[...] + lnb_ref[...]).astype(BF16)
        mixed = jnp.concatenate(
            [_dot(ws_ref[g], vn[:, g * ch:(g + 1) * ch]) for g in range(SGU_GROUPS)], axis=1)
        o_ref[rows, :] = (u_ref[rows, :] * (mixed + bias_ref[...]) * _silu(z_ref[rows, :])).astype(BF16)


def _na_prep_kernel(q_ref, k_ref, v_ref, cos_ref, sin_ref, qg_ref, kg_ref, ones_ref, qo_ref, ko_ref, vo_ref):
    cos = cos_ref[...]
    sin = sin_ref[...]
    ones = ones_ref[...]
    quarter = HEAD // 4
    low_quarter = (_iota((1, LANES), 1) % (2 * quarter)) < quarter

    def prep(x_ref, gain, scale, o_ref):
        for j in range(x_ref.shape[1] // LANES):
            sl = slice(j * LANES, (j + 1) * LANES)
            x = x_ref[:, sl]
            sq = x * x
            sq_hi = sq.astype(BF16)
            sq_lo = (sq - sq_hi.astype(F32)).astype(BF16)
            ms = (_dot(sq_hi, ones) + _dot(sq_lo, ones)) * (1.0 / HEAD)
            xn = x * lax.rsqrt(ms + NORM_EPS) * gain
            partner = jnp.where(low_quarter, pltpu.roll(xn, LANES - quarter, 1), pltpu.roll(xn, quarter, 1))
            o_ref[:, sl] = ((xn * cos + partner * sin) * scale).astype(BF16)

    prep(q_ref, qg_ref[...], HEAD ** -0.5, qo_ref)
    prep(k_ref, kg_ref[...], 1.0, ko_ref)
    vo_ref[...] = v_ref[...].astype(BF16)


def _rope_tables(seq, ctx_len):
    t = jnp.arange(seq)
    n_freq = HEAD // 4
    inv = ROPE_BASE ** (-jnp.arange(n_freq, dtype=F32) / n_freq)
    ang_r = (t // GRID_W).astype(F32)[:, None] * inv
    ang_c = (t % GRID_W).astype(F32)[:, None] * inv
    cos = jnp.concatenate([jnp.cos(ang_r), jnp.cos(ang_r), jnp.cos(ang_c), jnp.cos(ang_c)], axis=1)
    sin = jnp.concatenate([-jnp.sin(ang_r), jnp.sin(ang_r), -jnp.sin(ang_c), jnp.sin(ang_c)], axis=1)
    cos = jnp.concatenate([jnp.ones((ctx_len, HEAD), F32), cos], axis=0)
    sin = jnp.concatenate([jnp.zeros((ctx_len, HEAD), F32), sin], axis=0)
    reps = LANES // HEAD
    return jnp.tile(cos, (1, reps)), jnp.tile(sin, (1, reps))


_N_POOL_IN, _N_SGU_IN, _N_PREP_IN = 6, 7, 8


def _local_kernel(tb, nb, ctx_len, seq_all, *refs):
    ins, outs = refs[:-5], refs[-5:]
    pool_in = ins[:_N_POOL_IN]
    sgu_in = ins[_N_POOL_IN:_N_POOL_IN + _N_SGU_IN]
    prep_in = ins[_N_POOL_IN + _N_SGU_IN:]
    _pool_kernel(tb, nb, ctx_len, seq_all, *pool_in, outs[0])
    _sgu_kernel(*sgu_in, outs[1])
    _na_prep_kernel(*prep_in, *outs[2:])


def _local_mixers(p_main, cols, pool_w, pool_scale, ln_w, ln_b, w_s, b_s, cos, sin, q_gain, k_gain,
                  seq_all, ctx_len, gw):
    m = p_main.shape[0]
    tb = _largest_tile(ctx_len, SGU_CHUNK, 2 * SGU_CHUNK)
    assert ctx_len % tb == 0 and seq_all % tb == 0
    nb = seq_all // tb
    hb = tb // SUBLANES
    last = m // SUBLANES - 1
    reps = LANES // HEAD
    lane_head = jnp.arange(LANES) // HEAD
    ones = (lane_head[:, None] == lane_head[None, :]).astype(BF16)
    bias = jnp.repeat(b_s.T, gw // SGU_GROUPS, axis=1)
    tok = lambda col: pl.BlockSpec((tb, gw), lambda i: (i, col))
    vec = lambda w: pl.BlockSpec((1, w), lambda i: (0, 0))
    full = lambda arr: pl.BlockSpec(arr.shape, lambda i: (0,) * arr.ndim)
    tab = pl.BlockSpec((tb, LANES), lambda i: (i % nb, 0))
    pool_specs = [tok(cols["pool_v"]),
                  pl.BlockSpec((SUBLANES, gw), lambda i: (jnp.maximum(i * hb - 1, 0), cols["pool_v"])),
                  pl.BlockSpec((SUBLANES, gw), lambda i: (jnp.minimum((i + 1) * hb, last), cols["pool_v"])),
                  tok(cols["pool_z"]), full(pool_w), vec(gw)]
    pool_args = [p_main, p_main, p_main, p_main, pool_w, pool_scale.reshape(1, gw)]
    w_s = w_s.astype(BF16)
    sgu_specs = [tok(cols["sg_u"]), tok(cols["sg_v"]), tok(cols["sg_z"]), vec(gw), vec(gw), full(w_s), full(bias)]
    sgu_args = [p_main, p_main, p_main, ln_w.reshape(1, gw), ln_b.reshape(1, gw), w_s, bias]
    prep_specs = [tok(cols["na_q"]), tok(cols["na_k"]), tok(cols["na_v"]), tab, tab, vec(LANES), vec(LANES), full(ones)]
    prep_args = [p_main, p_main, p_main, cos, sin, jnp.tile(q_gain, reps).reshape(1, LANES),
                 jnp.tile(k_gain, reps).reshape(1, LANES), ones]
    assert (len(pool_specs), len(sgu_specs), len(prep_specs)) == (_N_POOL_IN, _N_SGU_IN, _N_PREP_IN)
    out = jax.ShapeDtypeStruct((m, gw), BF16)
    return pl.pallas_call(
        functools.partial(_local_kernel, tb, nb, ctx_len, seq_all),
        grid=(m // tb,),
        in_specs=pool_specs + sgu_specs + prep_specs,
        out_specs=[pl.BlockSpec((tb, gw), lambda i: (i, 0))] * 5,
        out_shape=[out] * 5,
        compiler_params=_params("parallel"),
        name="local_mixers",
    )(*pool_args, *sgu_args, *prep_args)


def _na_bias(rpb):
    depth, heads = rpb.shape[:2]
    qc = jnp.arange(GRID_W)[:, None]
    kc = jnp.arange(GRID_W)[None, :]
    cs = jnp.clip(qc - NA_KW // 2, 0, GRID_W - NA_KW)
    valid = (kc >= cs) & (kc < cs + NA_KW)
    rel_c = kc - qc + NA_KW - 1
    pick = ((rel_c[None] == jnp.arange(2 * NA_KW - 1)[:, None, None]) & valid[None]).astype(F32)
    by_offset = jnp.stack([rpb[:, :, NA_KH - 1 - d:2 * NA_KH - 1 - d, :] for d in range(NA_KH)], axis=1)
    t = jnp.einsum("ldhic,cqk->ldhqik", by_offset, pick, precision=lax.Precision.HIGHEST)
    t = jnp.where(valid[:, None, :], t, MASKED)
    reps = LANES // HEAD
    t = t.reshape(depth, NA_KH, heads // reps, reps * GRID_W, NA_KH * GRID_W)
    return jnp.concatenate([t, jnp.full((depth, 1) + t.shape[2:], MASKED, F32)], axis=1)


def _na_kernel(q_ref, kw_ref, vw_ref, kc_ref, vc_ref, bias_ref, z_ref, o_ref):
    head0 = _iota((1, LANES), 1) < HEAD
    rows = q_ref.shape[0]
    n_tiles = q_ref.shape[1] // LANES
    for g0 in range(0, n_tiles, NA_TILE_GROUP):
        ids = range(g0, min(g0 + NA_TILE_GROUP, n_tiles))
        tiles = {j: slice(j * LANES, (j + 1) * LANES) for j in ids}
        qs = {}
        for j in ids:
            q = q_ref[:, tiles[j]]
            zero = jnp.zeros_like(q)
            qs[j] = jnp.concatenate([jnp.where(head0, q, zero), jnp.where(head0, zero, q)], axis=0)
        s_w = {j: _dot_nt(qs[j], kw_ref[:, tiles[j]]) + bias_ref[0, 0, j] for j in ids}
        s_c = {j: _dot_nt(qs[j], kc_ref[:, tiles[j]]) for j in ids}
        mx = {j: jnp.maximum(jnp.max(s_w[j], axis=-1, keepdims=True), jnp.max(s_c[j], axis=-1, keepdims=True))
              for j in ids}
        p_w = {j: jnp.exp(s_w[j] - mx[j]) for j in ids}
        p_c = {j: jnp.exp(s_c[j] - mx[j]) for j in ids}
        den = {j: jnp.sum(p_w[j], axis=-1, keepdims=True) + jnp.sum(p_c[j], axis=-1, keepdims=True) for j in ids}
        o = {j: _dot(p_w[j].astype(BF16), vw_ref[:, tiles[j]]) + _dot(p_c[j].astype(BF16), vc_ref[:, tiles[j]])
             for j in ids}
        for j in ids:
            oj = o[j] / den[j]
            y = jnp.where(head0, oj[:rows], oj[rows:])
            o_ref[:, tiles[j]] = (y * _silu(z_ref[:, tiles[j]])).astype(BF16)


def _na(qn, kn, vb, p_main, bias_all, layer, batch, seq, ctx_len, col_z):
    m, gw = qn.shape
    seq_all = seq + ctx_len
    rows = seq // GRID_W
    assert rows >= NA_KH and ctx_len % GRID_W == 0 and seq_all % ctx_len == 0
    win = NA_KH * GRID_W

    def q_blk(b, r):
        return jnp.where(r < rows, (b * seq_all + ctx_len) // GRID_W + r, (b * seq_all) // GRID_W + r - rows)

    def win_start(b, r):
        start = b * seq_all + ctx_len + jnp.clip(r - NA_KH // 2, 0, rows - NA_KH) * GRID_W
        return pl.multiple_of(start, GRID_W)

    def bias_blk(b, r):
        return jnp.where(r < rows, r - jnp.clip(r - NA_KH // 2, 0, rows - NA_KH), NA_KH)

    window = pl.BlockSpec((pl.Element(win), pl.Element(gw)), lambda b, r: (win_start(b, r), 0))
    ctx_kv = pl.BlockSpec((ctx_len, gw), lambda b, r: (b * (seq_all // ctx_len), 0))
    return pl.pallas_call(
        _na_kernel,
        grid=(batch, rows + ctx_len // GRID_W),
        in_specs=[
            pl.BlockSpec((GRID_W, gw), lambda b, r: (q_blk(b, r), 0)),
            window, window, ctx_kv, ctx_kv,
            pl.BlockSpec((1, 1) + bias_all.shape[2:], lambda b, r: (layer, bias_blk(b, r), 0, 0, 0)),
            pl.BlockSpec((GRID_W, gw), lambda b, r: (q_blk(b, r), col_z)),
        ],
        out_specs=pl.BlockSpec((GRID_W, gw), lambda b, r: (q_blk(b, r), 0)),
        out_shape=jax.ShapeDtypeStruct((m, gw), BF16),
        compiler_params=_params("parallel", "arbitrary"),
        name="na_attention",
    )(qn, kn, vb, kn, vb, bias_all, p_main)


(_MU_R, _MU_K, _MU_V, _W0, _A0, _K_K, _K_A, _R_K, _LN_W, _LN_B) = range(10)


def _rwkv_kernel(reverse, chunk, n_ctx_blocks, r_ref, k_ref, v_ref, lo_ref, rh_ref, kh_ref, vh_ref, loh_ref,
                 vec_ref, mul_ref, wup_ref, aup_ref, *rest):
    if reverse:
        fwd_ref, z_ref, o_ref, state_ref = rest
    else:
        o_ref, state_ref = rest
    step = pl.program_id(1)
    rows = r_ref.shape[0]
    subs = rows // chunk

    @pl.when(step == 0)
    def _():
        state_ref[...] = jnp.zeros_like(state_ref)

    seq_start = jnp.logical_or(step == 0, step == n_ctx_blocks)
    row_id = _iota((rows, 1), 0)

    def lerp_shift(x_ref, halo_ref, mu):
        x = x_ref[...]
        if reverse:
            edge, rolled, edge_row = halo_ref[0:1, :], pltpu.roll(x, rows - 1, 0), rows - 1
        else:
            edge, rolled, edge_row = halo_ref[SUBLANES - 1:SUBLANES, :], pltpu.roll(x, 1, 0), 0
        edge = jnp.where(seq_start, jnp.zeros_like(edge), edge)
        return x + (jnp.where(row_id == edge_row, edge, rolled) - x) * mu

    vec = lambda n: vec_ref[n:n + 1, :]
    rl = lerp_shift(r_ref, rh_ref, vec(_MU_R))
    kl = lerp_shift(k_ref, kh_ref, vec(_MU_K))
    vl = lerp_shift(v_ref, vh_ref, vec(_MU_V))
    lol = lerp_shift(lo_ref, loh_ref, mul_ref[...])
    neg_w_pre = -(vec(_W0) + _dot(jnp.tanh(lol).astype(BF16), wup_ref[...]))
    w = -(jnp.maximum(neg_w_pre, 0.0) + jnp.log1p(jnp.exp(-jnp.abs(neg_w_pre)))) - 0.5
    ew = jnp.exp(w)
    a = jax.nn.sigmoid(vec(_A0) + _dot(lol.astype(BF16), aup_ref[...]))

    tt = _iota((rows, rows), 0)
    ss = _iota((rows, rows), 1)
    same_chunk = (tt ^ ss) < chunk
    tri = jnp.where(same_chunk & ((ss >= tt) if reverse else (ss <= tt)), 1.0, 0.0).astype(BF16)
    e_hi, e_mid, e_lo = _split3(ew)
    cs = _dot(tri, e_hi) + _dot(tri, e_mid) + _dot(tri, e_lo)
    kk_raw = kl * vec(_K_K)
    kt = kl * (1.0 + (a - 1.0) * vec(_K_A))
    w_incl = jnp.exp(-cs)
    w_excl = jnp.exp(ew - cs)
    w_inv = jnp.exp(cs)

    head0 = _iota((1, LANES), 1) < HEAD
    t_loc = _iota((chunk, LANES), 0)
    s_loc = _iota((chunk, LANES), 1) & (chunk - 1)
    earlier = (s_loc > t_loc) if reverse else (s_loc < t_loc)
    earlier_eq = (s_loc >= t_loc) if reverse else (s_loc <= t_loc)
    levels = chunk.bit_length() - 1
    later_half = 0 if reverse else 1
    sibling = [(((t_loc >> lb) ^ (s_loc >> lb)) == 1) & (((t_loc >> lb) & 1) == later_half)
               for lb in range(levels)]
    eye = jnp.where(s_loc == t_loc, 1.0, 0.0)
    lane_r = _iota((LANES, LANES), 0)
    lane_c = _iota((LANES, LANES), 1)
    lane_eye = lane_r == lane_c
    same_head = (lane_r ^ lane_c) < HEAD

    def stack(x):
        z = jnp.zeros_like(x)
        return jnp.concatenate([jnp.where(head0, x, z), jnp.where(head0, z, x)], axis=0)

    def head_sum(x):
        s0 = jnp.sum(jnp.where(head0, x, 0.0), axis=-1, keepdims=True)
        s1 = jnp.sum(jnp.where(head0, 0.0, x), axis=-1, keepdims=True)
        return jnp.where(head0, s0, s1)

    tiles = [slice(j * LANES, (j + 1) * LANES) for j in range(r_ref.shape[1] // LANES)]
    units = [(slice(u * chunk, (u + 1) * chunk), sl) for u in range(subs) for sl in tiles]
    kap, bhkh, rh32, rh, kw, bw, vb, vs, w_tot = [], [], [], [], [], [], [], [], []
    for ru, sl in units:
        kk = kk_raw[ru, sl]
        kk = kk * lax.rsqrt(jnp.maximum(head_sum(kk * kk), 1e-12))
        beta = kk * a[ru, sl]
        cs_u = cs[ru, sl]
        tot = cs_u[0:1, :] if reverse else cs_u[chunk - 1:chunk, :]
        w_rem = jnp.exp(cs_u - tot)
        w_tot.append(jnp.exp(-tot))
        kap.append((kk * w_excl[ru, sl]).astype(BF16))
        bhkh.append(jnp.concatenate([stack(beta * w_inv[ru, sl]), stack(kt[ru, sl] * w_inv[ru, sl])],
                                    axis=0).astype(BF16))
        rh32.append(rl[ru, sl] * w_incl[ru, sl])
        rh.append(rh32[-1].astype(BF16))
        kw.append((kt[ru, sl] * w_rem).astype(BF16))
        bw.append((beta * w_rem).astype(BF16))
        vb.append(vl[ru, sl].astype(BF16))
        vs.append(stack(vb[-1]))
    n = range(len(units))

    a_all = [_dot_nt(jnp.concatenate([kap[j], rh[j]], axis=0), bhkh[j]) for j in n]
    a_kb = [a_all[j][:chunk, :LANES] for j in n]
    a_rb = [jnp.where(earlier_eq, a_all[j][chunk:, :LANES], 0.0).astype(BF16) for j in n]
    a_xk = [jnp.concatenate([jnp.where(earlier, a_all[j][:chunk, LANES:], 0.0),
                             jnp.where(earlier_eq, a_all[j][chunk:, LANES:], 0.0)], axis=0).astype(BF16) for j in n]
    xk_v = [_dot(a_xk[j], vs[j]) for j in n]
    rhs = [xk_v[j][:chunk].astype(BF16) for j in n]

    inv = [eye - jnp.where(sibling[0], a_kb[j], 0.0) for j in n]
    for lb in range(1, levels):
        inv_b = [inv[j].astype(BF16) for j in n]
        half = [_dot(inv_b[j], stack(jnp.where(sibling[lb], a_kb[j], 0.0).astype(BF16))).astype(BF16) for j in n]
        inv = [inv[j] - _dot(half[j], stack(inv_b[j])) for j in n]
    inv_b = [inv[j].astype(BF16) for j in n]

    pq = [_dot(inv_b[j], jnp.concatenate([stack(kap[j]), stack(rhs[j])], axis=1)) for j in n]
    p_mat = [pq[j][:, :LANES].astype(BF16) for j in n]
    q_mat = [pq[j][:, LANES:].astype(BF16) for j in n]
    rb_pq = [_dot(a_rb[j], jnp.concatenate([stack(p_mat[j]), stack(q_mat[j])], axis=1)) for j in n]
    r_eff = [(rh32[j] - rb_pq[j][:, :LANES]).astype(BF16) for j in n]
    y0 = [xk_v[j][chunk:] - rb_pq[j][:, LANES:] for j in n]
    zero_b = jnp.zeros((chunk, LANES), BF16)
    upd = [_dot_tn(jnp.concatenate([kw[j], bw[j]], axis=0),
                   jnp.concatenate([jnp.concatenate([vb[j], zero_b], axis=1),
                                    jnp.concatenate([-q_mat[j], p_mat[j]], axis=1)], axis=0)) for j in n]
    h_add = [jnp.where(same_head, upd[j][:, :LANES], 0.0) for j in n]
    bp = [jnp.where(same_head, upd[j][:, LANES:], 0.0).astype(BF16) for j in n]

    w_col = [jnp.sum(jnp.where(lane_eye, w_tot[i], 0.0), axis=1, keepdims=True) for i in n]
    state = [state_ref[j] for j in range(len(tiles))]
    ys = [None] * len(units)
    for u in (range(subs - 1, -1, -1) if reverse else range(subs)):
        for j in range(len(tiles)):
            i = u * len(tiles) + j
            h_b = state[j].astype(BF16)
            ys[i] = _dot(r_eff[i], h_b) + y0[i]
            state[j] = state[j] * w_col[i] - _dot(bp[i], h_b) + h_add[i]
    for j in range(len(tiles)):
        state_ref[j] = state[j]

    for i, (ru, sl) in enumerate(units):
        y = ys[i]
        d = y - head_sum(y) * (1.0 / HEAD)
        var = head_sum(d * d) * (1.0 / HEAD)
        out = d * lax.rsqrt(var + GN_EPS) * vec(_LN_W)[:, sl] + vec(_LN_B)[:, sl]
        out = out + head_sum(rl[ru, sl] * kt[ru, sl] * vec(_R_K)[:, sl]) * vl[ru, sl]
        if reverse:
            o_ref[ru, sl] = ((fwd_ref[ru, sl] + out) * _silu(z_ref[ru, sl])).astype(BF16)
        else:
            o_ref[ru, sl] = out


def _rwkv_dir(reverse, p_main, p_lora, vecs, mu_lora, w_up, a_up, fwd_out, batch, seq_all, ctx_len,
              col_r, col_k, col_v, col_z, gw):
    m = p_main.shape[0]
    chunk = RW_CHUNK
    assert 2 * chunk == LANES and chunk == HEAD
    rows = RW_BLOCK_CHUNKS * chunk
    assert ctx_len % rows == 0 and seq_all % rows == 0
    n_blocks = seq_all // rows
    n_ctx = ctx_len // rows
    per8 = rows // SUBLANES
    last8 = m // SUBLANES - 1
    nl = p_lora.shape[1]

    def block_of(i):
        if reverse:
            return jnp.where(i < n_ctx, n_ctx - 1 - i, n_blocks - 1 + n_ctx - i)
        return i

    def blk(b, i):
        return b * n_blocks + block_of(i)

    def halo(b, i):
        if reverse:
            return jnp.minimum((blk(b, i) + 1) * per8, last8)
        return jnp.maximum(blk(b, i) * per8 - 1, 0)

    tok = lambda col: pl.BlockSpec((rows, gw), lambda b, i: (blk(b, i), col))
    hal = lambda col: pl.BlockSpec((SUBLANES, gw), lambda b, i: (halo(b, i), col))
    const = lambda arr: pl.BlockSpec(arr.shape, lambda b, i: (0,) * arr.ndim)
    in_specs = [tok(col_r), tok(col_k), tok(col_v), pl.BlockSpec((rows, nl), lambda b, i: (blk(b, i), 0)),
                hal(col_r), hal(col_k), hal(col_v), pl.BlockSpec((SUBLANES, nl), lambda b, i: (halo(b, i), 0)),
                const(vecs), const(mu_lora), const(w_up), const(a_up)]
    args = [p_main, p_main, p_main, p_lora, p_main, p_main, p_main, p_lora, vecs, mu_lora, w_up, a_up]
    if reverse:
        in_specs += [pl.BlockSpec((rows, gw), lambda b, i: (blk(b, i), 0)), tok(col_z)]
        args += [fwd_out, p_main]
    return pl.pallas_call(
        functools.partial(_rwkv_kernel, reverse, chunk, n_ctx),
        grid=(batch, n_blocks),
        in_specs=in_specs,
        out_specs=pl.BlockSpec((rows, gw), lambda b, i: (blk(b, i), 0)),
        out_shape=jax.ShapeDtypeStruct((m, gw), BF16 if reverse else F32),
        scratch_shapes=[pltpu.VMEM((gw // LANES, LANES, LANES), F32)],
        compiler_params=_params("parallel", "arbitrary"),
        name="rwkv_rev" if reverse else "rwkv_fwd",
    )(*args)


def _rwkv(p_main, p_lora, rw, batch, seq_all, ctx_len, cols, gw):
    mu_rkv, mu_lora, w0, w_up, a0, a_up, k_k, k_a, r_k, ln_w, ln_b = rw
    out = None
    for d, reverse in enumerate((False, True)):
        rows = [mu_rkv[d, 0], mu_rkv[d, 1], mu_rkv[d, 2], w0[d], a0[d], k_k, k_a, r_k.reshape(-1), ln_w, ln_b]
        vecs = jnp.zeros((2 * SUBLANES, gw), F32).at[:len(rows)].set(jnp.stack(rows))
        zeros = jnp.zeros((RW_LORA, gw), F32)
        parts_w = [zeros] * 4
        parts_a = [zeros] * 4
        parts_w[d] = w_up[d]
        parts_a[2 + d] = a_up[d]
        mu = jnp.zeros((4, RW_LORA), F32).at[d].set(mu_lora[d, 0]).at[2 + d].set(mu_lora[d, 1])
        out = _rwkv_dir(reverse, p_main, p_lora, vecs, mu.reshape(1, 4 * RW_LORA),
                        jnp.concatenate(parts_w).astype(BF16), jnp.concatenate(parts_a).astype(BF16),
                        out, batch, seq_all, ctx_len, *cols, gw)
    return out


def _merge_kernel(tm, nb, first_pos, ctx_len, y0_ref, y1_ref, y2_ref, y3_ref, w_ref, h_ref, gb_ref, gc_ref, o_ref):
    gw = y0_ref.shape[1]
    acc = _dot(y0_ref[...], w_ref[0, 0:gw, :])
    for g, y_ref in enumerate((y1_ref, y2_ref, y3_ref), start=1):
        acc += _dot(y_ref[...], w_ref[0, g * gw:(g + 1) * gw, :])
    pos = first_pos + (pl.program_id(0) % nb) * tm + _iota((tm, 1), 0)
    gate = jnp.where(pos < ctx_len, gc_ref[0], gb_ref[0])
    o_ref[...] = h_ref[...] + gate * acc


def _merge(ys, w_out_all, layer, h, mod3, batch, seq_all, ctx_len, tn, latent_only):
    d = h.shape[1]
    gw = ys[0].shape[1]
    first_pos = ctx_len if latent_only else 0
    rows = seq_all - first_pos
    tm = _largest_tile(rows, 2 * LANES, 1024)
    nb = rows // tm

    def row_start(i):
        return pl.multiple_of((i // nb) * seq_all + first_pos + (i % nb) * tm, 2 * LANES)

    y_spec = pl.BlockSpec((pl.Element(tm), pl.Element(gw)), lambda i, j: (row_start(i), 0))
    gate_col = 2 * (d // tn)
    return pl.pallas_call(
        functools.partial(_merge_kernel, tm, nb, first_pos, ctx_len),
        grid=(batch * nb, d // tn),
        in_specs=[y_spec, y_spec, y_spec, y_spec,
                  pl.BlockSpec((1, w_out_all.shape[1], tn), lambda i, j: (layer, 0, j)),
                  pl.BlockSpec((pl.Element(tm), pl.Element(tn)), lambda i, j: (row_start(i), j * tn)),
                  pl.BlockSpec((1, 1, tn), lambda i, j: (i // nb, 0, gate_col + j)),
                  pl.BlockSpec((1, 1, tn), lambda i, j: (batch, 0, gate_col + j))],
        out_specs=pl.BlockSpec((tm, tn), lambda i, j: (i, j)),
        out_shape=jax.ShapeDtypeStruct((batch * rows, d), F32),
        compiler_params=_params("parallel", "parallel"),
        name="merge",
    )(*ys, w_out_all, h, mod3, mod3)


def kernel(x, c, ctx, c_ctx, w_mod, b_mod, w_in, w_out, pool_w, pool_scale, na_q_gain, na_k_gain, na_rpb, rw_mu_rkv, rw_mu_lora, rw_w0, rw_w_up, rw_a0, rw_a_up, rw_k_k, rw_k_a, rw_r_k, rw_ln_w, rw_ln_b, sg_ln_w, sg_ln_b, sg_w, sg_b):
    batch, seq, d = x.shape
    ctx_len = ctx.shape[1]
    seq_all = ctx_len + seq
    m = batch * seq_all
    depth = w_mod.shape[0]
    gw = d // 4
    assert batch < SUBLANES and gw % LANES == 0

    h = jnp.concatenate([ctx, x], axis=1).reshape(m, d)
    cond = jnp.zeros((SUBLANES, d), F32).at[:batch].set(c).at[batch].set(c_ctx)
    mod = _modulation(cond, w_mod, b_mod)
    cos, sin = _rope_tables(seq, ctx_len)
    na_bias = _na_bias(na_rpb)
    tm = _largest_tile(seq_all, LANES, IN_PROJ_ROWS)
    tn = IN_PROJ_COLS

    lora_lo = 10 * gw
    lora_w = 4 * RW_LORA
    assert lora_lo % tn == 0 and gw % tn == 0
    w_out_b = w_out.astype(BF16)
    cols = {name: j for j, name in enumerate(("pool_v", "pool_z", "na_q", "na_k", "na_v", "na_z", "rw_r", "rw_k",
                                               "rw_v", "rw_z", "sg_u", "sg_v", "sg_z"))}

    for l in range(depth):
        mod3 = mod[l].reshape(SUBLANES, 1, 3 * d)
        xn = _norm_mod(h, mod3, batch, seq_all, ctx_len)
        p_main = _in_proj(xn, w_in, l, tm, tn, w_in.shape[2] - lora_w,
                          lambda j: jnp.where(j < lora_lo // tn, j * tn, j * tn + lora_w))
        p_lora = _in_proj(xn, w_in, l, tm, lora_w, lora_w, lambda j: lora_lo)

        y_pool, y_sg, qn, kn, vb = _local_mixers(
            p_main, cols, pool_w[l].astype(BF16), pool_scale[l], sg_ln_w[l], sg_ln_b[l], sg_w[l], sg_b[l],
            cos, sin, na_q_gain[l], na_k_gain[l], seq_all, ctx_len, gw)
        y_na = _na(qn, kn, vb, p_main, na_bias, l, batch, seq, ctx_len, cols["na_z"])
        rw = (rw_mu_rkv[l], rw_mu_lora[l], rw_w0[l], rw_w_up[l], rw_a0[l], rw_a_up[l],
              rw_k_k[l], rw_k_a[l], rw_r_k[l], rw_ln_w[l], rw_ln_b[l])
        y_rw = _rwkv(p_main, p_lora, rw, batch, seq_all, ctx_len, [cols[k] for k in ("rw_r", "rw_k", "rw_v", "rw_z")], gw)
        h = _merge((y_pool, y_na, y_rw, y_sg), w_out_b, l, h, mod3, batch, seq_all, ctx_len, gw,
                   latent_only=(l == depth - 1))

    return h.reshape(batch, seq, d)
```

```python
import functools

import jax
import jax.numpy as jnp
from jax import lax
from jax.experimental import pallas as pl
from jax.experimental.pallas import tpu as pltpu

F32 = jnp.float32
BF16 = jnp.bfloat16

LANES = 128
SUBLANES = 8
HEAD = 64
GRID_W = 64
POOL_WINDOWS = (2, 4, 8, 16)
NA_KH = 8
NA_KW = 16
NA_TILE_GROUP = 8
ROPE_BASE = 10000.0
RW_LORA = 64
RW_CHUNK = 64
RW_BLOCK_CHUNKS = 4
GN_EPS = 64e-5
SGU_CHUNK = 128
SGU_GROUPS = 8
LN_EPS = 1e-5
NORM_EPS = 1e-6
MASKED = -1e30
VMEM_LIMIT = 56 * 1024 * 1024
IN_PROJ_ROWS = 1408
IN_PROJ_COLS = 512


def _dot(a, b):
    return jnp.dot(a, b, preferred_element_type=F32)


def _dot_nt(a, b):
    return lax.dot_general(a, b, (((1,), (1,)), ((), ())), preferred_element_type=F32)


def _dot_tn(a, b):
    return lax.dot_general(a, b, (((0,), (0,)), ((), ())), preferred_element_type=F32)


def _silu(x):
    return x * jax.nn.sigmoid(x)


def _split3(x):
    hi = x.astype(BF16)
    r = x - hi.astype(F32)
    mid = r.astype(BF16)
    lo = (r - mid.astype(F32)).astype(BF16)
    return hi, mid, lo


def _iota(shape, dim):
    return lax.broadcasted_iota(jnp.int32, shape, dim)


def _params(*sem):
    return pltpu.CompilerParams(dimension_semantics=sem, vmem_limit_bytes=VMEM_LIMIT)


def _largest_tile(n, unit, cap):
    best = unit
    for t in range(unit, cap + 1, unit):
        if n % t == 0:
            best = t
    return best


def _mod_kernel(c_ref, w_ref, b_ref, o_ref):
    cs = _silu(c_ref[...]).astype(BF16)
    o_ref[0] = _dot(cs, w_ref[0].astype(BF16)) + b_ref[0]


def _modulation(cond, w_mod, b_mod):
    depth, d, n = w_mod.shape
    tn = 512
    return pl.pallas_call(
        _mod_kernel,
        grid=(depth, n // tn),
        in_specs=[
            pl.BlockSpec((SUBLANES, d), lambda l, j: (0, 0)),
            pl.BlockSpec((1, d, tn), lambda l, j: (l, 0, j)),
            pl.BlockSpec((1, 1, tn), lambda l, j: (l, 0, j)),
        ],
        out_specs=pl.BlockSpec((1, SUBLANES, tn), lambda l, j: (l, 0, j)),
        out_shape=jax.ShapeDtypeStruct((depth, SUBLANES, n), F32),
        compiler_params=_params("parallel", "parallel"),
        name="modulation",
    )(cond, w_mod, b_mod.reshape(depth, 1, n))


def _norm_kernel(nb, n_ctx_blocks, *refs):
    if len(refs) == 4:
        h_ref, shift_ref, scale_ref, o_ref = refs
        x = h_ref[...]
    else:
        ctx_ref, lat_ref, shift_ref, scale_ref, o_ref, slab_ref = refs
        x = jnp.where(pl.program_id(0) % nb < n_ctx_blocks, ctx_ref[...], lat_ref[...])
        slab_ref[...] = x
    xn = x * lax.rsqrt(jnp.mean(x * x, axis=-1, keepdims=True) + NORM_EPS)
    o_ref[...] = (xn * (1.0 + scale_ref[0]) + shift_ref[0]).astype(BF16)


def _norm_mod(streams, mod3, batch, seq_all, ctx_len):
    d = streams[0].shape[1]
    m = batch * seq_all
    tb = _largest_tile(ctx_len, SUBLANES, 256)
    nb = seq_all // tb
    ncb = ctx_len // tb

    def row(i):
        return jnp.where(i % nb < ncb, batch, i // nb)

    tok = pl.BlockSpec((tb, d), lambda i: (i, 0))
    if len(streams) == 1:
        stream_specs, outs, out_specs = [tok], jax.ShapeDtypeStruct((m, d), BF16), tok
    else:
        stream_specs = [pl.BlockSpec((tb, d), lambda i: ((i // nb) * ncb + jnp.minimum(i % nb, ncb - 1), 0)),
                        pl.BlockSpec((tb, d), lambda i: ((i // nb) * (nb - ncb) + jnp.maximum(i % nb - ncb, 0), 0))]
        outs = [jax.ShapeDtypeStruct((m, d), BF16), jax.ShapeDtypeStruct((m, d), F32)]
        out_specs = [tok, tok]
    return pl.pallas_call(
        functools.partial(_norm_kernel, nb, ncb),
        grid=(m // tb,),
        in_specs=stream_specs + [pl.BlockSpec((1, 1, d), lambda i: (row(i), 0, 0)),
                                 pl.BlockSpec((1, 1, d), lambda i: (row(i), 0, 1))],
        out_specs=out_specs,
        out_shape=outs,
        compiler_params=_params("parallel"),
        name="norm_mod",
    )(*streams, mod3, mod3)


def _mm_kernel(a_ref, w_ref, o_ref, wb_ref):
    @pl.when(pl.program_id(1) == 0)
    def _():
        wb_ref[...] = w_ref[0].astype(BF16)

    o_ref[...] = _dot(a_ref[...], wb_ref[...])


def _in_proj(a, w_all, layer, tm, tn, n_out, col_start):
    m, k = a.shape
    w_spec = pl.BlockSpec((pl.Element(1), pl.Element(k), pl.Element(tn)),
                          lambda j, i: (layer, 0, pl.multiple_of(col_start(j), LANES)))
    return pl.pallas_call(
        _mm_kernel,
        grid=(n_out // tn, m // tm),
        in_specs=[pl.BlockSpec((tm, k), lambda j, i: (i, 0)), w_spec],
        out_specs=pl.BlockSpec((tm, tn), lambda j, i: (i, j)),
        out_shape=jax.ShapeDtypeStruct((m, n_out), F32),
        scratch_shapes=[pltpu.VMEM((k, tn), BF16)],
        compiler_params=_params("arbitrary", "arbitrary"),
        name="in_proj",
    )(a, w_all)


def _pool_kernel(tb, nb, ctx_len, seq_all, v_ref, vp_ref, vn_ref, z_ref, pw_ref, ps_ref, o_ref):
    s0 = (pl.program_id(0) % nb) * tb
    is_ctx = s0 < ctx_len
    seq_lo = jnp.where(is_ctx, 0, ctx_len) - s0
    seq_hi = jnp.where(is_ctx, ctx_len, seq_all) - s0
    v = v_ref[...]
    ext = jnp.concatenate([vp_ref[...], v, vn_ref[...]], axis=0)
    halo = SUBLANES
    t = _iota((tb, tb + 2 * halo), 0)
    e = _iota((tb, tb + 2 * halo), 1) - halo
    t1 = _iota((tb, 1), 0)
    pool_ch = v.shape[1] // len(POOL_WINDOWS)
    outs = []
    for g, win in enumerate(POOL_WINDOWS):
        lo = jnp.maximum(t - win // 2, seq_lo)
        hi = jnp.minimum(t - win // 2 + win, seq_hi)
        band = jnp.where((e >= lo) & (e < hi), 1.0, 0.0).astype(BF16)
        cnt = (jnp.minimum(t1 - win // 2 + win, seq_hi) - jnp.maximum(t1 - win // 2, seq_lo)).astype(F32)
        ch = slice(g * pool_ch, (g + 1) * pool_ch)
        p_hi, p_mid, p_lo = _split3(ext[:, ch])
        mean = (_dot(band, p_hi) + _dot(band, p_mid) + _dot(band, p_lo)) / cnt
        outs.append(_dot((mean - v[:, ch]).astype(BF16), pw_ref[g]))
    y = jnp.concatenate(outs, axis=1) * ps_ref[...]
    o_ref[...] = (y * _silu(z_ref[...])).astype(BF16)


def _sgu_kernel(u_ref, v_ref, z_ref, lnw_ref, lnb_ref, ws_ref, bias_ref, o_ref):
    ch = v_ref.shape[1] // SGU_GROUPS
    for c0 in range(0, v_ref.shape[0], SGU_CHUNK):
        rows = slice(c0, c0 + SGU_CHUNK)
        v = v_ref[rows, :]
        d = v - jnp.mean(v, axis=-1, keepdims=True)
        var = jnp.mean(d * d, axis=-1, keepdims=True)
        vn = (d * lax.rsqrt(var + LN_EPS) * lnw_ref[...] + lnb_ref[...]).astype(BF16)
        mixed = jnp.concatenate(
            [_dot(ws_ref[g], vn[:, g * ch:(g + 1) * ch]) for g in range(SGU_GROUPS)], axis=1)
        o_ref[rows, :] = (u_ref[rows, :] * (mixed + bias_ref[...]) * _silu(z_ref[rows, :])).astype(BF16)


def _na_prep_kernel(q_ref, k_ref, v_ref, cos_ref, sin_ref, qg_ref, kg_ref, ones_ref, qo_ref, ko_ref, vo_ref):
    cos = cos_ref[...]
    sin = sin_ref[...]
    ones = ones_ref[...]
    quarter = HEAD // 4
    low_quarter = (_iota((1, LANES), 1) % (2 * quarter)) < quarter

    def prep(x_ref, gain, scale, o_ref):
        for j in range(x_ref.shape[1] // LANES):
            sl = slice(j * LANES, (j + 1) * LANES)
            x = x_ref[:, sl]
            sq = x * x
            sq_hi = sq.astype(BF16)
            sq_lo = (sq - sq_hi.astype(F32)).astype(BF16)
            ms = (_dot(sq_hi, ones) + _dot(sq_lo, ones)) * (1.0 / HEAD)
            xn = x * lax.rsqrt(ms + NORM_EPS) * gain
            partner = jnp.where(low_quarter, pltpu.roll(xn, LANES - quarter, 1), pltpu.roll(xn, quarter, 1))
            o_ref[:, sl] = ((xn * cos + partner * sin) * scale).astype(BF16)

    prep(q_ref, qg_ref[...], HEAD ** -0.5, qo_ref)
    prep(k_ref, kg_ref[...], 1.0, ko_ref)
    vo_ref[...] = v_ref[...].astype(BF16)


def _rope_tables(seq, ctx_len):
    t = jnp.arange(seq)
    n_freq = HEAD // 4
    inv = ROPE_BASE ** (-jnp.arange(n_freq, dtype=F32) / n_freq)
    ang_r = (t // GRID_W).astype(F32)[:, None] * inv
    ang_c = (t % GRID_W).astype(F32)[:, None] * inv
    cos = jnp.concatenate([jnp.cos(ang_r), jnp.cos(ang_r), jnp.cos(ang_c), jnp.cos(ang_c)], axis=1)
    sin = jnp.concatenate([-jnp.sin(ang_r), jnp.sin(ang_r), -jnp.sin(ang_c), jnp.sin(ang_c)], axis=1)
    cos = jnp.concatenate([jnp.ones((ctx_len, HEAD), F32), cos], axis=0)
    sin = jnp.concatenate([jnp.zeros((ctx_len, HEAD), F32), sin], axis=0)
    reps = LANES // HEAD
    return jnp.tile(cos, (1, reps)), jnp.tile(sin, (1, reps))


_N_POOL_IN, _N_SGU_IN, _N_PREP_IN = 6, 7, 8


def _local_kernel(tb, nb, ctx_len, seq_all, *refs):
    ins, outs = refs[:-5], refs[-5:]
    pool_in = ins[:_N_POOL_IN]
    sgu_in = ins[_N_POOL_IN:_N_POOL_IN + _N_SGU_IN]
    prep_in = ins[_N_POOL_IN + _N_SGU_IN:]
    _pool_kernel(tb, nb, ctx_len, seq_all, *pool_in, outs[0])
    _sgu_kernel(*sgu_in, outs[1])
    _na_prep_kernel(*prep_in, *outs[2:])


def _local_mixers(p_main, cols, pool_w, pool_scale, ln_w, ln_b, w_s, b_s, cos, sin, q_gain, k_gain,
                  seq_all, ctx_len, gw):
    m = p_main.shape[0]
    tb = _largest_tile(ctx_len, SGU_CHUNK, 2 * SGU_CHUNK)
    assert ctx_len % tb == 0 and seq_all % tb == 0
    nb = seq_all // tb
    hb = tb // SUBLANES
    last = m // SUBLANES - 1
    reps = LANES // HEAD
    lane_head = jnp.arange(LANES) // HEAD
    ones = (lane_head[:, None] == lane_head[None, :]).astype(BF16)
    bias = jnp.repeat(b_s.T, gw // SGU_GROUPS, axis=1)
    tok = lambda col: pl.BlockSpec((tb, gw), lambda i: (i, col))
    vec = lambda w: pl.BlockSpec((1, w), lambda i: (0, 0))
    full = lambda arr: pl.BlockSpec(arr.shape, lambda i: (0,) * arr.ndim)
    tab = pl.BlockSpec((tb, LANES), lambda i: (i % nb, 0))
    pool_specs = [tok(cols["pool_v"]),
                  pl.BlockSpec((SUBLANES, gw), lambda i: (jnp.maximum(i * hb - 1, 0), cols["pool_v"])),
                  pl.BlockSpec((SUBLANES, gw), lambda i: (jnp.minimum((i + 1) * hb, last), cols["pool_v"])),
                  tok(cols["pool_z"]), full(pool_w), vec(gw)]
    pool_args = [p_main, p_main, p_main, p_main, pool_w, pool_scale.reshape(1, gw)]
    w_s = w_s.astype(BF16)
    sgu_specs = [tok(cols["sg_u"]), tok(cols["sg_v"]), tok(cols["sg_z"]), vec(gw), vec(gw), full(w_s), full(bias)]
    sgu_args = [p_main, p_main, p_main, ln_w.reshape(1, gw), ln_b.reshape(1, gw), w_s, bias]
    prep_specs = [tok(cols["na_q"]), tok(cols["na_k"]), tok(cols["na_v"]), tab, tab, vec(LANES), vec(LANES), full(ones)]
    prep_args = [p_main, p_main, p_main, cos, sin, jnp.tile(q_gain, reps).reshape(1, LANES),
                 jnp.tile(k_gain, reps).reshape(1, LANES), ones]
    assert (len(pool_specs), len(sgu_specs), len(prep_specs)) == (_N_POOL_IN, _N_SGU_IN, _N_PREP_IN)
    out = jax.ShapeDtypeStruct((m, gw), BF16)
    return pl.pallas_call(
        functools.partial(_local_kernel, tb, nb, ctx_len, seq_all),
        grid=(m // tb,),
        in_specs=pool_specs + sgu_specs + prep_specs,
        out_specs=[pl.BlockSpec((tb, gw), lambda i: (i, 0))] * 5,
        out_shape=[out] * 5,
        compiler_params=_params("parallel"),
        name="local_mixers",
    )(*pool_args, *sgu_args, *prep_args)


def _na_bias(rpb):
    depth, heads = rpb.shape[:2]
    qc = jnp.arange(GRID_W)[:, None]
    kc = jnp.arange(GRID_W)[None, :]
    cs = jnp.clip(qc - NA_KW // 2, 0, GRID_W - NA_KW)
    valid = (kc >= cs) & (kc < cs + NA_KW)
    rel_c = kc - qc + NA_KW - 1
    pick = ((rel_c[None] == jnp.arange(2 * NA_KW - 1)[:, None, None]) & valid[None]).astype(F32)
    by_offset = jnp.stack([rpb[:, :, NA_KH - 1 - d:2 * NA_KH - 1 - d, :] for d in range(NA_KH)], axis=1)
    t = jnp.einsum("ldhic,cqk->ldhqik", by_offset, pick, precision=lax.Precision.HIGHEST)
    t = jnp.where(valid[:, None, :], t, MASKED)
    reps = LANES // HEAD
    t = t.reshape(depth, NA_KH, heads // reps, reps * GRID_W, NA_KH * GRID_W)
    return jnp.concatenate([t, jnp.full((depth, 1) + t.shape[2:], MASKED, F32)], axis=1)


def _na_kernel(q_ref, kw_ref, vw_ref, kc_ref, vc_ref, bias_ref, z_ref, o_ref):
    head0 = _iota((1, LANES), 1) < HEAD
    rows = q_ref.shape[0]
    n_tiles = q_ref.shape[1] // LANES
    for g0 in range(0, n_tiles, NA_TILE_GROUP):
        ids = range(g0, min(g0 + NA_TILE_GROUP, n_tiles))
        tiles = {j: slice(j * LANES, (j + 1) * LANES) for j in ids}
        qs = {}
        for j in ids:
            q = q_ref[:, tiles[j]]
            zero = jnp.zeros_like(q)
            qs[j] = jnp.concatenate([jnp.where(head0, q, zero), jnp.where(head0, zero, q)], axis=0)
        s_w = {j: _dot_nt(qs[j], kw_ref[:, tiles[j]]) + bias_ref[0, 0, j] for j in ids}
        s_c = {j: _dot_nt(qs[j], kc_ref[:, tiles[j]]) for j in ids}
        mx = {j: jnp.maximum(jnp.max(s_w[j], axis=-1, keepdims=True), jnp.max(s_c[j], axis=-1, keepdims=True))
              for j in ids}
        p_w = {j: jnp.exp(s_w[j] - mx[j]) for j in ids}
        p_c = {j: jnp.exp(s_c[j] - mx[j]) for j in ids}
        den = {j: jnp.sum(p_w[j], axis=-1, keepdims=True) + jnp.sum(p_c[j], axis=-1, keepdims=True) for j in ids}
        o = {j: _dot(p_w[j].astype(BF16), vw_ref[:, tiles[j]]) + _dot(p_c[j].astype(BF16), vc_ref[:, tiles[j]])
             for j in ids}
        for j in ids:
            oj = o[j] / den[j]
            y = jnp.where(head0, oj[:rows], oj[rows:])
            o_ref[:, tiles[j]] = (y * _silu(z_ref[:, tiles[j]])).astype(BF16)


def _na(qn, kn, vb, p_main, bias_all, layer, batch, seq, ctx_len, col_z):
    m, gw = qn.shape
    seq_all = seq + ctx_len
    rows = seq // GRID_W
    assert rows >= NA_KH and ctx_len % GRID_W == 0 and seq_all % ctx_len == 0
    win = NA_KH * GRID_W

    def q_blk(b, r):
        return jnp.where(r < rows, (b * seq_all + ctx_len) // GRID_W + r, (b * seq_all) // GRID_W + r - rows)

    def win_start(b, r):
        start = b * seq_all + ctx_len + jnp.clip(r - NA_KH // 2, 0, rows - NA_KH) * GRID_W
        return pl.multiple_of(start, GRID_W)

    def bias_blk(b, r):
        return jnp.where(r < rows, r - jnp.clip(r - NA_KH // 2, 0, rows - NA_KH), NA_KH)

    window = pl.BlockSpec((pl.Element(win), pl.Element(gw)), lambda b, r: (win_start(b, r), 0))
    ctx_kv = pl.BlockSpec((ctx_len, gw), lambda b, r: (b * (seq_all // ctx_len), 0))
    return pl.pallas_call(
        _na_kernel,
        grid=(batch, rows + ctx_len // GRID_W),
        in_specs=[
            pl.BlockSpec((GRID_W, gw), lambda b, r: (q_blk(b, r), 0)),
            window, window, ctx_kv, ctx_kv,
            pl.BlockSpec((1, 1) + bias_all.shape[2:], lambda b, r: (layer, bias_blk(b, r), 0, 0, 0)),
            pl.BlockSpec((GRID_W, gw), lambda b, r: (q_blk(b, r), col_z)),
        ],
        out_specs=pl.BlockSpec((GRID_W, gw), lambda b, r: (q_blk(b, r), 0)),
        out_shape=jax.ShapeDtypeStruct((m, gw), BF16),
        compiler_params=_params("parallel", "arbitrary"),
        name="na_attention",
    )(qn, kn, vb, kn, vb, bias_all, p_main)


(_MU_R, _MU_K, _MU_V, _W0, _A0, _K_K, _K_A, _R_K, _LN_W, _LN_B) = range(10)


def _rwkv_kernel(reverse, chunk, n_ctx_blocks, r_ref, k_ref, v_ref, lo_ref, rh_ref, kh_ref, vh_ref, loh_ref,
                 vec_ref, mul_ref, wup_ref, aup_ref, *rest):
    if reverse:
        fwd_ref, z_ref, o_ref, state_ref = rest
    else:
        o_ref, state_ref = rest
    step = pl.program_id(1)
    rows = r_ref.shape[0]
    subs = rows // chunk

    @pl.when(step == 0)
    def _():
        state_ref[...] = jnp.zeros_like(state_ref)

    seq_start = jnp.logical_or(step == 0, step == n_ctx_blocks)
    row_id = _iota((rows, 1), 0)

    def lerp_shift(x_ref, halo_ref, mu):
        x = x_ref[...]
        if reverse:
            edge, rolled, edge_row = halo_ref[0:1, :], pltpu.roll(x, rows - 1, 0), rows - 1
        else:
            edge, rolled, edge_row = halo_ref[SUBLANES - 1:SUBLANES, :], pltpu.roll(x, 1, 0), 0
        edge = jnp.where(seq_start, jnp.zeros_like(edge), edge)
        return x + (jnp.where(row_id == edge_row, edge, rolled) - x) * mu

    vec = lambda n: vec_ref[n:n + 1, :]
    rl = lerp_shift(r_ref, rh_ref, vec(_MU_R))
    kl = lerp_shift(k_ref, kh_ref, vec(_MU_K))
    vl = lerp_shift(v_ref, vh_ref, vec(_MU_V))
    lol = lerp_shift(lo_ref, loh_ref, mul_ref[...])
    neg_w_pre = -(vec(_W0) + _dot(jnp.tanh(lol).astype(BF16), wup_ref[...]))
    w = -(jnp.maximum(neg_w_pre, 0.0) + jnp.log(1.0 + jnp.exp(-jnp.abs(neg_w_pre)))) - 0.5
    ew = jnp.exp(w)
    a = jax.nn.sigmoid(vec(_A0) + _dot(lol.astype(BF16), aup_ref[...]))

    tt = _iota((rows, rows), 0)
    ss = _iota((rows, rows), 1)
    same_chunk = (tt ^ ss) < chunk
    tri = jnp.where(same_chunk & ((ss >= tt) if reverse else (ss <= tt)), 1.0, 0.0).astype(BF16)
    e_hi, e_mid, e_lo = _split3(ew)
    cs = _dot(tri, e_hi) + _dot(tri, e_mid) + _dot(tri, e_lo)
    kk_raw = kl * vec(_K_K)
    kt = kl * (1.0 + (a - 1.0) * vec(_K_A))
    w_incl = jnp.exp(-cs)
    w_excl = jnp.exp(ew - cs)
    w_inv = jnp.exp(cs)

    head0 = _iota((1, LANES), 1) < HEAD
    t_loc = _iota((chunk, LANES), 0)
    s_loc = _iota((chunk, LANES), 1) & (chunk - 1)
    earlier = (s_loc > t_loc) if reverse else (s_loc < t_loc)
    earlier_eq = (s_loc >= t_loc) if reverse else (s_loc <= t_loc)
    levels = chunk.bit_length() - 1
    later_half = 0 if reverse else 1
    sibling = [(((t_loc >> lb) ^ (s_loc >> lb)) == 1) & (((t_loc >> lb) & 1) == later_half)
               for lb in range(levels)]
    eye = jnp.where(s_loc == t_loc, 1.0, 0.0)
    lane_r = _iota((LANES, LANES), 0)
    lane_c = _iota((LANES, LANES), 1)
    lane_eye = lane_r == lane_c
    same_head = (lane_r ^ lane_c) < HEAD

    def stack(x):
        z = jnp.zeros_like(x)
        return jnp.concatenate([jnp.where(head0, x, z), jnp.where(head0, z, x)], axis=0)

    def head_sum(x):
        s0 = jnp.sum(jnp.where(head0, x, 0.0), axis=-1, keepdims=True)
        s1 = jnp.sum(jnp.where(head0, 0.0, x), axis=-1, keepdims=True)
        return jnp.where(head0, s0, s1)

    tiles = [slice(j * LANES, (j + 1) * LANES) for j in range(r_ref.shape[1] // LANES)]
    units = [(slice(u * chunk, (u + 1) * chunk), sl) for u in range(subs) for sl in tiles]
    kap, bhkh, rh32, rh, kw, bw, vb, vs, w_tot = [], [], [], [], [], [], [], [], []
    for ru, sl in units:
        kk = kk_raw[ru, sl]
        kk = kk * lax.rsqrt(jnp.maximum(head_sum(kk * kk), 1e-12))
        beta = kk * a[ru, sl]
        cs_u = cs[ru, sl]
        tot = cs_u[0:1, :] if reverse else cs_u[chunk - 1:chunk, :]
        w_rem = jnp.exp(cs_u - tot)
        w_tot.append(jnp.exp(-tot))
        kap.append((kk * w_excl[ru, sl]).astype(BF16))
        bhkh.append(jnp.concatenate([stack(beta * w_inv[ru, sl]), stack(kt[ru, sl] * w_inv[ru, sl])],
                                    axis=0).astype(BF16))
        rh32.append(rl[ru, sl] * w_incl[ru, sl])
        rh.append(rh32[-1].astype(BF16))
        kw.append((kt[ru, sl] * w_rem).astype(BF16))
        bw.append((beta * w_rem).astype(BF16))
        vb.append(vl[ru, sl].astype(BF16))
        vs.append(stack(vb[-1]))
    n = range(len(units))

    a_all = [_dot_nt(jnp.concatenate([kap[j], rh[j]], axis=0), bhkh[j]) for j in n]
    a_kb = [a_all[j][:chunk, :LANES] for j in n]
    a_rb = [jnp.where(earlier_eq, a_all[j][chunk:, :LANES], 0.0).astype(BF16) for j in n]
    a_xk = [jnp.concatenate([jnp.where(earlier, a_all[j][:chunk, LANES:], 0.0),
                             jnp.where(earlier_eq, a_all[j][chunk:, LANES:], 0.0)], axis=0).astype(BF16) for j in n]
    xk_v = [_dot(a_xk[j], vs[j]) for j in n]
    rhs = [xk_v[j][:chunk].astype(BF16) for j in n]

    inv = [eye - jnp.where(sibling[0], a_kb[j], 0.0) for j in n]
    for lb in range(1, levels):
        inv_b = [inv[j].astype(BF16) for j in n]
        half = [_dot(inv_b[j], stack(jnp.where(sibling[lb], a_kb[j], 0.0).astype(BF16))).astype(BF16) for j in n]
        inv = [inv[j] - _dot(half[j], stack(inv_b[j])) for j in n]
    inv_b = [inv[j].astype(BF16) for j in n]

    pq = [_dot(inv_b[j], jnp.concatenate([stack(kap[j]), stack(rhs[j])], axis=1)) for j in n]
    p_mat = [pq[j][:, :LANES].astype(BF16) for j in n]
    q_mat = [pq[j][:, LANES:].astype(BF16) for j in n]
    rb_pq = [_dot(a_rb[j], jnp.concatenate([stack(p_mat[j]), stack(q_mat[j])], axis=1)) for j in n]
    r_eff = [(rh32[j] - rb_pq[j][:, :LANES]).astype(BF16) for j in n]
    y0 = [xk_v[j][chunk:] - rb_pq[j][:, LANES:] for j in n]
    zero_b = jnp.zeros((chunk, LANES), BF16)
    upd = [_dot_tn(jnp.concatenate([kw[j], bw[j]], axis=0),
                   jnp.concatenate([jnp.concatenate([vb[j], zero_b], axis=1),
                                    jnp.concatenate([-q_mat[j], p_mat[j]], axis=1)], axis=0)) for j in n]
    h_add = [jnp.where(same_head, upd[j][:, :LANES], 0.0) for j in n]
    bp = [jnp.where(same_head, upd[j][:, LANES:], 0.0).astype(BF16) for j in n]

    w_col = [jnp.sum(jnp.where(lane_eye, w_tot[i], 0.0), axis=1, keepdims=True) for i in n]
    state = [state_ref[j] for j in range(len(tiles))]
    ys = [None] * len(units)
    for u in (range(subs - 1, -1, -1) if reverse else range(subs)):
        for j in range(len(tiles)):
            i = u * len(tiles) + j
            h_b = state[j].astype(BF16)
            ys[i] = _dot(r_eff[i], h_b) + y0[i]
            state[j] = state[j] * w_col[i] - _dot(bp[i], h_b) + h_add[i]
    for j in range(len(tiles)):
        state_ref[j] = state[j]

    for i, (ru, sl) in enumerate(units):
        y = ys[i]
        d = y - head_sum(y) * (1.0 / HEAD)
        var = head_sum(d * d) * (1.0 / HEAD)
        out = d * lax.rsqrt(var + GN_EPS) * vec(_LN_W)[:, sl] + vec(_LN_B)[:, sl]
        out = out + head_sum(rl[ru, sl] * kt[ru, sl] * vec(_R_K)[:, sl]) * vl[ru, sl]
        if reverse:
            o_ref[ru, sl] = ((fwd_ref[ru, sl] + out) * _silu(z_ref[ru, sl])).astype(BF16)
        else:
            o_ref[ru, sl] = out


def _rwkv_dir(reverse, p_main, p_lora, vecs, mu_lora, w_up, a_up, fwd_out, batch, seq_all, ctx_len,
              col_r, col_k, col_v, col_z, gw):
    m = p_main.shape[0]
    chunk = RW_CHUNK
    assert 2 * chunk == LANES and chunk == HEAD
    rows = RW_BLOCK_CHUNKS * chunk
    assert ctx_len % rows == 0 and seq_all % rows == 0
    n_blocks = seq_all // rows
    n_ctx = ctx_len // rows
    per8 = rows // SUBLANES
    last8 = m // SUBLANES - 1
    nl = p_lora.shape[1]

    def block_of(i):
        if reverse:
            return jnp.where(i < n_ctx, n_ctx - 1 - i, n_blocks - 1 + n_ctx - i)
        return i

    def blk(b, i):
        return b * n_blocks + block_of(i)

    def halo(b, i):
        if reverse:
            return jnp.minimum((blk(b, i) + 1) * per8, last8)
        return jnp.maximum(blk(b, i) * per8 - 1, 0)

    tok = lambda col: pl.BlockSpec((rows, gw), lambda b, i: (blk(b, i), col))
    hal = lambda col: pl.BlockSpec((SUBLANES, gw), lambda b, i: (halo(b, i), col))
    const = lambda arr: pl.BlockSpec(arr.shape, lambda b, i: (0,) * arr.ndim)
    in_specs = [tok(col_r), tok(col_k), tok(col_v), pl.BlockSpec((rows, nl), lambda b, i: (blk(b, i), 0)),
                hal(col_r), hal(col_k), hal(col_v), pl.BlockSpec((SUBLANES, nl), lambda b, i: (halo(b, i), 0)),
                const(vecs), const(mu_lora), const(w_up), const(a_up)]
    args = [p_main, p_main, p_main, p_lora, p_main, p_main, p_main, p_lora, vecs, mu_lora, w_up, a_up]
    if reverse:
        in_specs += [pl.BlockSpec((rows, gw), lambda b, i: (blk(b, i), 0)), tok(col_z)]
        args += [fwd_out, p_main]
    return pl.pallas_call(
        functools.partial(_rwkv_kernel, reverse, chunk, n_ctx),
        grid=(batch, n_blocks),
        in_specs=in_specs,
        out_specs=pl.BlockSpec((rows, gw), lambda b, i: (blk(b, i), 0)),
        out_shape=jax.ShapeDtypeStruct((m, gw), BF16 if reverse else F32),
        scratch_shapes=[pltpu.VMEM((gw // LANES, LANES, LANES), F32)],
        compiler_params=_params("parallel", "arbitrary"),
        name="rwkv_rev" if reverse else "rwkv_fwd",
    )(*args)


def _rwkv(p_main, p_lora, rw, batch, seq_all, ctx_len, cols, gw):
    mu_rkv, mu_lora, w0, w_up, a0, a_up, k_k, k_a, r_k, ln_w, ln_b = rw
    out = None
    for d, reverse in enumerate((False, True)):
        rows = [mu_rkv[d, 0], mu_rkv[d, 1], mu_rkv[d, 2], w0[d], a0[d], k_k, k_a, r_k.reshape(-1), ln_w, ln_b]
        vecs = jnp.zeros((2 * SUBLANES, gw), F32).at[:len(rows)].set(jnp.stack(rows))
        zeros = jnp.zeros((RW_LORA, gw), F32)
        parts_w = [zeros] * 4
        parts_a = [zeros] * 4
        parts_w[d] = w_up[d]
        parts_a[2 + d] = a_up[d]
        mu = jnp.zeros((4, RW_LORA), F32).at[d].set(mu_lora[d, 0]).at[2 + d].set(mu_lora[d, 1])
        out = _rwkv_dir(reverse, p_main, p_lora, vecs, mu.reshape(1, 4 * RW_LORA),
                        jnp.concatenate(parts_w).astype(BF16), jnp.concatenate(parts_a).astype(BF16),
                        out, batch, seq_all, ctx_len, *cols, gw)
    return out


def _merge_kernel(tm, nb, first_pos, ctx_len, y0_ref, y1_ref, y2_ref, y3_ref, w_ref, h_ref, gb_ref, gc_ref, o_ref):
    gw = y0_ref.shape[1]
    acc = _dot(y0_ref[...], w_ref[0, 0:gw, :])
    for g, y_ref in enumerate((y1_ref, y2_ref, y3_ref), start=1):
        acc += _dot(y_ref[...], w_ref[0, g * gw:(g + 1) * gw, :])
    pos = first_pos + (pl.program_id(0) % nb) * tm + _iota((tm, 1), 0)
    gate = jnp.where(pos < ctx_len, gc_ref[0], gb_ref[0])
    o_ref[...] = h_ref[...] + gate * acc


def _merge(ys, w_out_all, layer, h, mod3, batch, seq_all, ctx_len, tn, latent_only):
    d = h.shape[1]
    gw = ys[0].shape[1]
    first_pos = ctx_len if latent_only else 0
    rows = seq_all - first_pos
    tm = _largest_tile(rows, 2 * LANES, 1024)
    nb = rows // tm

    def row_start(i):
        return pl.multiple_of((i // nb) * seq_all + first_pos + (i % nb) * tm, 2 * LANES)

    y_spec = pl.BlockSpec((pl.Element(tm), pl.Element(gw)), lambda i, j: (row_start(i), 0))
    gate_col = 2 * (d // tn)
    return pl.pallas_call(
        functools.partial(_merge_kernel, tm, nb, first_pos, ctx_len),
        grid=(batch * nb, d // tn),
        in_specs=[y_spec, y_spec, y_spec, y_spec,
                  pl.BlockSpec((1, w_out_all.shape[1], tn), lambda i, j: (layer, 0, j)),
                  pl.BlockSpec((pl.Element(tm), pl.Element(tn)), lambda i, j: (row_start(i), j * tn)),
                  pl.BlockSpec((1, 1, tn), lambda i, j: (i // nb, 0, gate_col + j)),
                  pl.BlockSpec((1, 1, tn), lambda i, j: (batch, 0, gate_col + j))],
        out_specs=pl.BlockSpec((tm, tn), lambda i, j: (i, j)),
        out_shape=jax.ShapeDtypeStruct((batch * rows, d), F32),
        compiler_params=_params("parallel", "parallel"),
        name="merge",
    )(*ys, w_out_all, h, mod3, mod3)


def kernel(x, c, ctx, c_ctx, w_mod, b_mod, w_in, w_out, pool_w, pool_scale, na_q_gain, na_k_gain, na_rpb, rw_mu_rkv, rw_mu_lora, rw_w0, rw_w_up, rw_a0, rw_a_up, rw_k_k, rw_k_a, rw_r_k, rw_ln_w, rw_ln_b, sg_ln_w, sg_ln_b, sg_w, sg_b):
    batch, seq, d = x.shape
    ctx_len = ctx.shape[1]
    seq_all = ctx_len + seq
    m = batch * seq_all
    depth = w_mod.shape[0]
    gw = d // 4
    assert batch < SUBLANES and gw % LANES == 0

    h = None
    cond = jnp.zeros((SUBLANES, d), F32).at[:batch].set(c).at[batch].set(c_ctx)
    mod = _modulation(cond, w_mod, b_mod)
    cos, sin = _rope_tables(seq, ctx_len)
    na_bias = _na_bias(na_rpb)
    tm = _largest_tile(seq_all, LANES, IN_PROJ_ROWS)
    tn = IN_PROJ_COLS

    lora_lo = 10 * gw
    lora_w = 4 * RW_LORA
    assert lora_lo % tn == 0 and gw % tn == 0
    w_out_b = w_out.astype(BF16)
    cols = {name: j for j, name in enumerate(("pool_v", "pool_z", "na_q", "na_k", "na_v", "na_z", "rw_r", "rw_k",
                                               "rw_v", "rw_z", "sg_u", "sg_v", "sg_z"))}

    for l in range(depth):
        mod3 = mod[l].reshape(SUBLANES, 1, 3 * d)
        if l == 0:
            xn, h = _norm_mod((ctx.reshape(batch * ctx_len, d), x.reshape(batch * seq, d)), mod3,
                              batch, seq_all, ctx_len)
        else:
            xn = _norm_mod((h,), mod3, batch, seq_all, ctx_len)
        p_main = _in_proj(xn, w_in, l, tm, tn, w_in.shape[2] - lora_w,
                          lambda j: jnp.where(j < lora_lo // tn, j * tn, j * tn + lora_w))
        p_lora = _in_proj(xn, w_in, l, tm, lora_w, lora_w, lambda j: lora_lo)

        y_pool, y_sg, qn, kn, vb = _local_mixers(
            p_main, cols, pool_w[l].astype(BF16), pool_scale[l], sg_ln_w[l], sg_ln_b[l], sg_w[l], sg_b[l],
            cos, sin, na_q_gain[l], na_k_gain[l], seq_all, ctx_len, gw)
        y_na = _na(qn, kn, vb, p_main, na_bias, l, batch, seq, ctx_len, cols["na_z"])
        rw = (rw_mu_rkv[l], rw_mu_lora[l], rw_w0[l], rw_w_up[l], rw_a0[l], rw_a_up[l],
              rw_k_k[l], rw_k_a[l], rw_r_k[l], rw_ln_w[l], rw_ln_b[l])
        y_rw = _rwkv(p_main, p_lora, rw, batch, seq_all, ctx_len, [cols[k] for k in ("rw_r", "rw_k", "rw_v", "rw_z")], gw)
        h = _merge((y_pool, y_na, y_rw, y_sg), w_out_b, l, h, mod3, batch, seq_all, ctx_len, gw,
                   latent_only=(l == depth - 1))

    return h.reshape(batch, seq, d)
```

```python
import functools

import jax
import jax.numpy as jnp
from jax import lax
from jax.experimental import pallas as pl
from jax.experimental.pallas import tpu as pltpu

F32 = jnp.float32
BF16 = jnp.bfloat16

LANES = 128
SUBLANES = 8
HEAD = 64
GRID_W = 64
POOL_WINDOWS = (2, 4, 8, 16)
NA_KH = 8
NA_KW = 16
NA_TILE_GROUP = 8
ROPE_BASE = 10000.0
RW_LORA = 64
RW_CHUNK = 64
RW_BLOCK_CHUNKS = 4
GN_EPS = 64e-5
SGU_CHUNK = 128
SGU_GROUPS = 8
LN_EPS = 1e-5
NORM_EPS = 1e-6
MASKED = -1e30
VMEM_LIMIT = 56 * 1024 * 1024
IN_PROJ_ROWS = 1408
IN_PROJ_COLS = 512


def _dot(a, b):
    return jnp.dot(a, b, preferred_element_type=F32)


def _dot_nt(a, b):
    return lax.dot_general(a, b, (((1,), (1,)), ((), ())), preferred_element_type=F32)


def _dot_tn(a, b):
    return lax.dot_general(a, b, (((0,), (0,)), ((), ())), preferred_element_type=F32)


def _silu(x):
    return x * jax.nn.sigmoid(x)


def _split3(x):
    hi = x.astype(BF16)
    r = x - hi.astype(F32)
    mid = r.astype(BF16)
    lo = (r - mid.astype(F32)).astype(BF16)
    return hi, mid, lo


def _iota(shape, dim):
    return lax.broadcasted_iota(jnp.int32, shape, dim)


def _params(*sem):
    return pltpu.CompilerParams(dimension_semantics=sem, vmem_limit_bytes=VMEM_LIMIT)


def _largest_tile(n, unit, cap):
    best = unit
    for t in range(unit, cap + 1, unit):
        if n % t == 0:
            best = t
    return best


def _mod_kernel(c_ref, w_ref, b_ref, o_ref):
    cs = _silu(c_ref[...]).astype(BF16)
    o_ref[0] = _dot(cs, w_ref[0].astype(BF16)) + b_ref[0]


def _modulation(cond, w_mod, b_mod):
    depth, d, n = w_mod.shape
    tn = 512
    return pl.pallas_call(
        _mod_kernel,
        grid=(depth, n // tn),
        in_specs=[
            pl.BlockSpec((SUBLANES, d), lambda l, j: (0, 0)),
            pl.BlockSpec((1, d, tn), lambda l, j: (l, 0, j)),
            pl.BlockSpec((1, 1, tn), lambda l, j: (l, 0, j)),
        ],
        out_specs=pl.BlockSpec((1, SUBLANES, tn), lambda l, j: (l, 0, j)),
        out_shape=jax.ShapeDtypeStruct((depth, SUBLANES, n), F32),
        compiler_params=_params("parallel", "parallel"),
        name="modulation",
    )(cond, w_mod, b_mod.reshape(depth, 1, n))


def _norm_kernel(nb, n_ctx_blocks, *refs):
    if len(refs) == 4:
        h_ref, shift_ref, scale_ref, o_ref = refs
        x = h_ref[...]
    else:
        ctx_ref, lat_ref, shift_ref, scale_ref, o_ref, slab_ref = refs
        x = jnp.where(pl.program_id(0) % nb < n_ctx_blocks, ctx_ref[...], lat_ref[...])
        slab_ref[...] = x
    xn = x * lax.rsqrt(jnp.mean(x * x, axis=-1, keepdims=True) + NORM_EPS)
    o_ref[...] = (xn * (1.0 + scale_ref[0]) + shift_ref[0]).astype(BF16)


def _norm_mod(streams, mod3, batch, seq_all, ctx_len):
    d = streams[0].shape[1]
    m = batch * seq_all
    tb = _largest_tile(ctx_len, SUBLANES, 256)
    nb = seq_all // tb
    ncb = ctx_len // tb

    def row(i):
        return jnp.where(i % nb < ncb, batch, i // nb)

    tok = pl.BlockSpec((tb, d), lambda i: (i, 0))
    if len(streams) == 1:
        stream_specs, outs, out_specs = [tok], jax.ShapeDtypeStruct((m, d), BF16), tok
    else:
        stream_specs = [pl.BlockSpec((tb, d), lambda i: ((i // nb) * ncb + jnp.minimum(i % nb, ncb - 1), 0)),
                        pl.BlockSpec((tb, d), lambda i: ((i // nb) * (nb - ncb) + jnp.maximum(i % nb - ncb, 0), 0))]
        outs = [jax.ShapeDtypeStruct((m, d), BF16), jax.ShapeDtypeStruct((m, d), F32)]
        out_specs = [tok, tok]
    return pl.pallas_call(
        functools.partial(_norm_kernel, nb, ncb),
        grid=(m // tb,),
        in_specs=stream_specs + [pl.BlockSpec((1, 1, d), lambda i: (row(i), 0, 0)),
                                 pl.BlockSpec((1, 1, d), lambda i: (row(i), 0, 1))],
        out_specs=out_specs,
        out_shape=outs,
        compiler_params=_params("parallel"),
        name="norm_mod",
    )(*streams, mod3, mod3)


def _mm_kernel(a_ref, w_ref, o_ref, wb_ref):
    @pl.when(pl.program_id(1) == 0)
    def _():
        wb_ref[...] = w_ref[0].astype(BF16)

    o_ref[...] = _dot(a_ref[...], wb_ref[...])


def _in_proj(a, w_all, layer, tm, tn, n_out, col_start):
    m, k = a.shape
    w_spec = pl.BlockSpec((pl.Element(1), pl.Element(k), pl.Element(tn)),
                          lambda j, i: (layer, 0, pl.multiple_of(col_start(j), LANES)))
    return pl.pallas_call(
        _mm_kernel,
        grid=(n_out // tn, m // tm),
        in_specs=[pl.BlockSpec((tm, k), lambda j, i: (i, 0)), w_spec],
        out_specs=pl.BlockSpec((tm, tn), lambda j, i: (i, j)),
        out_shape=jax.ShapeDtypeStruct((m, n_out), F32),
        scratch_shapes=[pltpu.VMEM((k, tn), BF16)],
        compiler_params=_params("arbitrary", "arbitrary"),
        name="in_proj",
    )(a, w_all)


def _pool_kernel(tb, nb, ctx_len, seq_all, v_ref, vp_ref, vn_ref, z_ref, pw_ref, ps_ref, o_ref):
    s0 = (pl.program_id(0) % nb) * tb
    is_ctx = s0 < ctx_len
    seq_lo = jnp.where(is_ctx, 0, ctx_len) - s0
    seq_hi = jnp.where(is_ctx, ctx_len, seq_all) - s0
    v = v_ref[...]
    ext = jnp.concatenate([vp_ref[...], v, vn_ref[...]], axis=0)
    halo = SUBLANES
    t = _iota((tb, tb + 2 * halo), 0)
    e = _iota((tb, tb + 2 * halo), 1) - halo
    t1 = _iota((tb, 1), 0)
    pool_ch = v.shape[1] // len(POOL_WINDOWS)
    outs = []
    for g, win in enumerate(POOL_WINDOWS):
        lo = jnp.maximum(t - win // 2, seq_lo)
        hi = jnp.minimum(t - win // 2 + win, seq_hi)
        band = jnp.where((e >= lo) & (e < hi), 1.0, 0.0).astype(BF16)
        cnt = (jnp.minimum(t1 - win // 2 + win, seq_hi) - jnp.maximum(t1 - win // 2, seq_lo)).astype(F32)
        ch = slice(g * pool_ch, (g + 1) * pool_ch)
        p_hi = ext[:, ch].astype(BF16)
        p_lo = (ext[:, ch] - p_hi.astype(F32)).astype(BF16)
        mean = (_dot(band, p_hi) + _dot(band, p_lo)) / cnt
        outs.append(_dot((mean - v[:, ch]).astype(BF16), pw_ref[g]))
    y = jnp.concatenate(outs, axis=1) * ps_ref[...]
    o_ref[...] = (y * _silu(z_ref[...])).astype(BF16)


def _sgu_kernel(u_ref, v_ref, z_ref, lnw_ref, lnb_ref, ws_ref, bias_ref, o_ref):
    ch = v_ref.shape[1] // SGU_GROUPS
    for c0 in range(0, v_ref.shape[0], SGU_CHUNK):
        rows = slice(c0, c0 + SGU_CHUNK)
        v = v_ref[rows, :]
        d = v - jnp.mean(v, axis=-1, keepdims=True)
        var = jnp.mean(d * d, axis=-1, keepdims=True)
        vn = (d * lax.rsqrt(var + LN_EPS) * lnw_ref[...] + lnb_ref[...]).astype(BF16)
        mixed = jnp.concatenate(
            [_dot(ws_ref[g], vn[:, g * ch:(g + 1) * ch]) for g in range(SGU_GROUPS)], axis=1)
        o_ref[rows, :] = (u_ref[rows, :] * (mixed + bias_ref[...]) * _silu(z_ref[rows, :])).astype(BF16)


def _na_prep_kernel(q_ref, k_ref, v_ref, cos_ref, sin_ref, qg_ref, kg_ref, ones_ref, qo_ref, ko_ref, vo_ref):
    cos = cos_ref[...]
    sin = sin_ref[...]
    ones = ones_ref[...]
    quarter = HEAD // 4
    low_quarter = (_iota((1, LANES), 1) % (2 * quarter)) < quarter

    def prep(x_ref, gain, scale, o_ref):
        for j in range(x_ref.shape[1] // LANES):
            sl = slice(j * LANES, (j + 1) * LANES)
            x = x_ref[:, sl]
            sq = x * x
            sq_hi = sq.astype(BF16)
            sq_lo = (sq - sq_hi.astype(F32)).astype(BF16)
            ms = (_dot(sq_hi, ones) + _dot(sq_lo, ones)) * (1.0 / HEAD)
            xn = x * lax.rsqrt(ms + NORM_EPS) * gain
            partner = jnp.where(low_quarter, pltpu.roll(xn, LANES - quarter, 1), pltpu.roll(xn, quarter, 1))
            o_ref[:, sl] = ((xn * cos + partner * sin) * scale).astype(BF16)

    prep(q_ref, qg_ref[...], HEAD ** -0.5, qo_ref)
    prep(k_ref, kg_ref[...], 1.0, ko_ref)
    vo_ref[...] = v_ref[...].astype(BF16)


def _rope_tables(seq, ctx_len):
    t = jnp.arange(seq)
    n_freq = HEAD // 4
    inv = ROPE_BASE ** (-jnp.arange(n_freq, dtype=F32) / n_freq)
    ang_r = (t // GRID_W).astype(F32)[:, None] * inv
    ang_c = (t % GRID_W).astype(F32)[:, None] * inv
    cos = jnp.concatenate([jnp.cos(ang_r), jnp.cos(ang_r), jnp.cos(ang_c), jnp.cos(ang_c)], axis=1)
    sin = jnp.concatenate([-jnp.sin(ang_r), jnp.sin(ang_r), -jnp.sin(ang_c), jnp.sin(ang_c)], axis=1)
    cos = jnp.concatenate([jnp.ones((ctx_len, HEAD), F32), cos], axis=0)
    sin = jnp.concatenate([jnp.zeros((ctx_len, HEAD), F32), sin], axis=0)
    reps = LANES // HEAD
    return jnp.tile(cos, (1, reps)), jnp.tile(sin, (1, reps))


_N_POOL_IN, _N_SGU_IN, _N_PREP_IN = 6, 7, 8


def _local_kernel(tb, nb, ctx_len, seq_all, *refs):
    ins, outs = refs[:-5], refs[-5:]
    pool_in = ins[:_N_POOL_IN]
    sgu_in = ins[_N_POOL_IN:_N_POOL_IN + _N_SGU_IN]
    prep_in = ins[_N_POOL_IN + _N_SGU_IN:]
    _pool_kernel(tb, nb, ctx_len, seq_all, *pool_in, outs[0])
    _sgu_kernel(*sgu_in, outs[1])
    _na_prep_kernel(*prep_in, *outs[2:])


def _local_mixers(p_main, cols, pool_w, pool_scale, ln_w, ln_b, w_s, b_s, cos, sin, q_gain, k_gain,
                  seq_all, ctx_len, gw):
    m = p_main.shape[0]
    tb = _largest_tile(ctx_len, SGU_CHUNK, 2 * SGU_CHUNK)
    assert ctx_len % tb == 0 and seq_all % tb == 0
    nb = seq_all // tb
    hb = tb // SUBLANES
    last = m // SUBLANES - 1
    reps = LANES // HEAD
    lane_head = jnp.arange(LANES) // HEAD
    ones = (lane_head[:, None] == lane_head[None, :]).astype(BF16)
    bias = jnp.repeat(b_s.T, gw // SGU_GROUPS, axis=1)
    tok = lambda col: pl.BlockSpec((tb, gw), lambda i: (i, col))
    vec = lambda w: pl.BlockSpec((1, w), lambda i: (0, 0))
    full = lambda arr: pl.BlockSpec(arr.shape, lambda i: (0,) * arr.ndim)
    tab = pl.BlockSpec((tb, LANES), lambda i: (i % nb, 0))
    pool_specs = [tok(cols["pool_v"]),
                  pl.BlockSpec((SUBLANES, gw), lambda i: (jnp.maximum(i * hb - 1, 0), cols["pool_v"])),
                  pl.BlockSpec((SUBLANES, gw), lambda i: (jnp.minimum((i + 1) * hb, last), cols["pool_v"])),
                  tok(cols["pool_z"]), full(pool_w), vec(gw)]
    pool_args = [p_main, p_main, p_main, p_main, pool_w, pool_scale.reshape(1, gw)]
    w_s = w_s.astype(BF16)
    sgu_specs = [tok(cols["sg_u"]), tok(cols["sg_v"]), tok(cols["sg_z"]), vec(gw), vec(gw), full(w_s), full(bias)]
    sgu_args = [p_main, p_main, p_main, ln_w.reshape(1, gw), ln_b.reshape(1, gw), w_s, bias]
    prep_specs = [tok(cols["na_q"]), tok(cols["na_k"]), tok(cols["na_v"]), tab, tab, vec(LANES), vec(LANES), full(ones)]
    prep_args = [p_main, p_main, p_main, cos, sin, jnp.tile(q_gain, reps).reshape(1, LANES),
                 jnp.tile(k_gain, reps).reshape(1, LANES), ones]
    assert (len(pool_specs), len(sgu_specs), len(prep_specs)) == (_N_POOL_IN, _N_SGU_IN, _N_PREP_IN)
    out = jax.ShapeDtypeStruct((m, gw), BF16)
    return pl.pallas_call(
        functools.partial(_local_kernel, tb, nb, ctx_len, seq_all),
        grid=(m // tb,),
        in_specs=pool_specs + sgu_specs + prep_specs,
        out_specs=[pl.BlockSpec((tb, gw), lambda i: (i, 0))] * 5,
        out_shape=[out] * 5,
        compiler_params=_params("parallel"),
        name="local_mixers",
    )(*pool_args, *sgu_args, *prep_args)


def _na_bias(rpb):
    depth, heads = rpb.shape[:2]
    qc = jnp.arange(GRID_W)[:, None]
    kc = jnp.arange(GRID_W)[None, :]
    cs = jnp.clip(qc - NA_KW // 2, 0, GRID_W - NA_KW)
    valid = (kc >= cs) & (kc < cs + NA_KW)
    rel_c = kc - qc + NA_KW - 1
    pick = ((rel_c[None] == jnp.arange(2 * NA_KW - 1)[:, None, None]) & valid[None]).astype(F32)
    by_offset = jnp.stack([rpb[:, :, NA_KH - 1 - d:2 * NA_KH - 1 - d, :] for d in range(NA_KH)], axis=1)
    t = jnp.einsum("ldhic,cqk->ldhqik", by_offset, pick, precision=lax.Precision.HIGHEST)
    t = jnp.where(valid[:, None, :], t, MASKED)
    reps = LANES // HEAD
    t = t.reshape(depth, NA_KH, heads // reps, reps * GRID_W, NA_KH * GRID_W)
    return jnp.concatenate([t, jnp.full((depth, 1) + t.shape[2:], MASKED, F32)], axis=1)


def _na_kernel(q_ref, kw_ref, vw_ref, kc_ref, vc_ref, bias_ref, z_ref, o_ref):
    head0 = _iota((1, LANES), 1) < HEAD
    rows = q_ref.shape[0]
    n_tiles = q_ref.shape[1] // LANES
    for g0 in range(0, n_tiles, NA_TILE_GROUP):
        ids = range(g0, min(g0 + NA_TILE_GROUP, n_tiles))
        tiles = {j: slice(j * LANES, (j + 1) * LANES) for j in ids}
        qs = {}
        for j in ids:
            q = q_ref[:, tiles[j]]
            zero = jnp.zeros_like(q)
            qs[j] = jnp.concatenate([jnp.where(head0, q, zero), jnp.where(head0, zero, q)], axis=0)
        s_w = {j: _dot_nt(qs[j], kw_ref[:, tiles[j]]) + bias_ref[0, 0, j] for j in ids}
        s_c = {j: _dot_nt(qs[j], kc_ref[:, tiles[j]]) for j in ids}
        mx = {j: jnp.maximum(jnp.max(s_w[j], axis=-1, keepdims=True), jnp.max(s_c[j], axis=-1, keepdims=True))
              for j in ids}
        p_w = {j: jnp.exp(s_w[j] - mx[j]) for j in ids}
        p_c = {j: jnp.exp(s_c[j] - mx[j]) for j in ids}
        den = {j: jnp.sum(p_w[j], axis=-1, keepdims=True) + jnp.sum(p_c[j], axis=-1, keepdims=True) for j in ids}
        o = {j: _dot(p_w[j].astype(BF16), vw_ref[:, tiles[j]]) + _dot(p_c[j].astype(BF16), vc_ref[:, tiles[j]])
             for j in ids}
        for j in ids:
            oj = o[j] / den[j]
            y = jnp.where(head0, oj[:rows], oj[rows:])
            o_ref[:, tiles[j]] = (y * _silu(z_ref[:, tiles[j]])).astype(BF16)


def _na(qn, kn, vb, p_main, bias_all, layer, batch, seq, ctx_len, col_z):
    m, gw = qn.shape
    seq_all = seq + ctx_len
    rows = seq // GRID_W
    assert rows >= NA_KH and ctx_len % GRID_W == 0 and seq_all % ctx_len == 0
    win = NA_KH * GRID_W

    def q_blk(b, r):
        return jnp.where(r < rows, (b * seq_all + ctx_len) // GRID_W + r, (b * seq_all) // GRID_W + r - rows)

    def win_start(b, r):
        start = b * seq_all + ctx_len + jnp.clip(r - NA_KH // 2, 0, rows - NA_KH) * GRID_W
        return pl.multiple_of(start, GRID_W)

    def bias_blk(b, r):
        return jnp.where(r < rows, r - jnp.clip(r - NA_KH // 2, 0, rows - NA_KH), NA_KH)

    window = pl.BlockSpec((pl.Element(win), pl.Element(gw)), lambda b, r: (win_start(b, r), 0))
    ctx_kv = pl.BlockSpec((ctx_len, gw), lambda b, r: (b * (seq_all // ctx_len), 0))
    return pl.pallas_call(
        _na_kernel,
        grid=(batch, rows + ctx_len // GRID_W),
        in_specs=[
            pl.BlockSpec((GRID_W, gw), lambda b, r: (q_blk(b, r), 0)),
            window, window, ctx_kv, ctx_kv,
            pl.BlockSpec((1, 1) + bias_all.shape[2:], lambda b, r: (layer, bias_blk(b, r), 0, 0, 0)),
            pl.BlockSpec((GRID_W, gw), lambda b, r: (q_blk(b, r), col_z)),
        ],
        out_specs=pl.BlockSpec((GRID_W, gw), lambda b, r: (q_blk(b, r), 0)),
        out_shape=jax.ShapeDtypeStruct((m, gw), BF16),
        compiler_params=_params("parallel", "arbitrary"),
        name="na_attention",
    )(qn, kn, vb, kn, vb, bias_all, p_main)


(_MU_R, _MU_K, _MU_V, _W0, _A0, _K_K, _K_A, _R_K, _LN_W, _LN_B) = range(10)


def _rwkv_kernel(reverse, chunk, n_ctx_blocks, r_ref, k_ref, v_ref, lo_ref, rh_ref, kh_ref, vh_ref, loh_ref,
                 vec_ref, mul_ref, wup_ref, aup_ref, *rest):
    if reverse:
        fwd_ref, z_ref, o_ref, state_ref = rest
    else:
        o_ref, state_ref = rest
    step = pl.program_id(1)
    rows = r_ref.shape[0]
    subs = rows // chunk

    @pl.when(step == 0)
    def _():
        state_ref[...] = jnp.zeros_like(state_ref)

    seq_start = jnp.logical_or(step == 0, step == n_ctx_blocks)
    row_id = _iota((rows, 1), 0)

    def lerp_shift(x_ref, halo_ref, mu):
        x = x_ref[...]
        if reverse:
            edge, rolled, edge_row = halo_ref[0:1, :], pltpu.roll(x, rows - 1, 0), rows - 1
        else:
            edge, rolled, edge_row = halo_ref[SUBLANES - 1:SUBLANES, :], pltpu.roll(x, 1, 0), 0
        edge = jnp.where(seq_start, jnp.zeros_like(edge), edge)
        return x + (jnp.where(row_id == edge_row, edge, rolled) - x) * mu

    vec = lambda n: vec_ref[n:n + 1, :]
    rl = lerp_shift(r_ref, rh_ref, vec(_MU_R))
    kl = lerp_shift(k_ref, kh_ref, vec(_MU_K))
    vl = lerp_shift(v_ref, vh_ref, vec(_MU_V))
    lol = lerp_shift(lo_ref, loh_ref, mul_ref[...])
    neg_w_pre = -(vec(_W0) + _dot(jnp.tanh(lol).astype(BF16), wup_ref[...]))
    w = -(jnp.maximum(neg_w_pre, 0.0) + jnp.log(1.0 + jnp.exp(-jnp.abs(neg_w_pre)))) - 0.5
    ew = jnp.exp(w)
    a = jax.nn.sigmoid(vec(_A0) + _dot(lol.astype(BF16), aup_ref[...]))

    tt = _iota((rows, rows), 0)
    ss = _iota((rows, rows), 1)
    same_chunk = (tt ^ ss) < chunk
    tri = jnp.where(same_chunk & ((ss >= tt) if reverse else (ss <= tt)), 1.0, 0.0).astype(BF16)
    e_hi, e_mid, e_lo = _split3(ew)
    cs = _dot(tri, e_hi) + _dot(tri, e_mid) + _dot(tri, e_lo)
    kk_raw = kl * vec(_K_K)
    kt = kl * (1.0 + (a - 1.0) * vec(_K_A))
    w_incl = jnp.exp(-cs)
    w_excl = jnp.exp(ew - cs)
    w_inv = jnp.exp(cs)

    head0 = _iota((1, LANES), 1) < HEAD
    t_loc = _iota((chunk, LANES), 0)
    s_loc = _iota((chunk, LANES), 1) & (chunk - 1)
    earlier = (s_loc > t_loc) if reverse else (s_loc < t_loc)
    earlier_eq = (s_loc >= t_loc) if reverse else (s_loc <= t_loc)
    levels = chunk.bit_length() - 1
    later_half = 0 if reverse else 1
    sibling = [(((t_loc >> lb) ^ (s_loc >> lb)) == 1) & (((t_loc >> lb) & 1) == later_half)
               for lb in range(levels)]
    eye = jnp.where(s_loc == t_loc, 1.0, 0.0)
    lane_r = _iota((LANES, LANES), 0)
    lane_c = _iota((LANES, LANES), 1)
    lane_eye = lane_r == lane_c
    same_head = (lane_r ^ lane_c) < HEAD

    def stack(x):
        z = jnp.zeros_like(x)
        return jnp.concatenate([jnp.where(head0, x, z), jnp.where(head0, z, x)], axis=0)

    def head_sum(x):
        s0 = jnp.sum(jnp.where(head0, x, 0.0), axis=-1, keepdims=True)
        s1 = jnp.sum(jnp.where(head0, 0.0, x), axis=-1, keepdims=True)
        return jnp.where(head0, s0, s1)

    tiles = [slice(j * LANES, (j + 1) * LANES) for j in range(r_ref.shape[1] // LANES)]
    units = [(slice(u * chunk, (u + 1) * chunk), sl) for u in range(subs) for sl in tiles]
    kap, bhkh, rh32, rh, kw, bw, vb, vs, w_tot = [], [], [], [], [], [], [], [], []
    for ru, sl in units:
        kk = kk_raw[ru, sl]
        kk = kk * lax.rsqrt(jnp.maximum(head_sum(kk * kk), 1e-12))
        beta = kk * a[ru, sl]
        cs_u = cs[ru, sl]
        tot = cs_u[0:1, :] if reverse else cs_u[chunk - 1:chunk, :]
        w_rem = jnp.exp(cs_u - tot)
        w_tot.append(jnp.exp(-tot))
        kap.append((kk * w_excl[ru, sl]).astype(BF16))
        bhkh.append(jnp.concatenate([stack(beta * w_inv[ru, sl]), stack(kt[ru, sl] * w_inv[ru, sl])],
                                    axis=0).astype(BF16))
        rh32.append(rl[ru, sl] * w_incl[ru, sl])
        rh.append(rh32[-1].astype(BF16))
        kw.append((kt[ru, sl] * w_rem).astype(BF16))
        bw.append((beta * w_rem).astype(BF16))
        vb.append(vl[ru, sl].astype(BF16))
        vs.append(stack(vb[-1]))
    n = range(len(units))

    a_all = [_dot_nt(jnp.concatenate([kap[j], rh[j]], axis=0), bhkh[j]) for j in n]
    a_kb = [a_all[j][:chunk, :LANES] for j in n]
    a_rb = [jnp.where(earlier_eq, a_all[j][chunk:, :LANES], 0.0).astype(BF16) for j in n]
    a_xk = [jnp.concatenate([jnp.where(earlier, a_all[j][:chunk, LANES:], 0.0),
                             jnp.where(earlier_eq, a_all[j][chunk:, LANES:], 0.0)], axis=0).astype(BF16) for j in n]
    xk_v = [_dot(a_xk[j], vs[j]) for j in n]
    rhs = [xk_v[j][:chunk].astype(BF16) for j in n]

    inv = [eye - jnp.where(sibling[0], a_kb[j], 0.0) for j in n]
    for lb in range(1, levels):
        inv_b = [inv[j].astype(BF16) for j in n]
        half = [_dot(inv_b[j], stack(jnp.where(sibling[lb], a_kb[j], 0.0).astype(BF16))).astype(BF16) for j in n]
        inv = [inv[j] - _dot(half[j], stack(inv_b[j])) for j in n]
    inv_b = [inv[j].astype(BF16) for j in n]

    pq = [_dot(inv_b[j], jnp.concatenate([stack(kap[j]), stack(rhs[j])], axis=1)) for j in n]
    p_mat = [pq[j][:, :LANES].astype(BF16) for j in n]
    q_mat = [pq[j][:, LANES:].astype(BF16) for j in n]
    rb_pq = [_dot(a_rb[j], jnp.concatenate([stack(p_mat[j]), stack(q_mat[j])], axis=1)) for j in n]
    r_eff = [(rh32[j] - rb_pq[j][:, :LANES]).astype(BF16) for j in n]
    y0 = [xk_v[j][chunk:] - rb_pq[j][:, LANES:] for j in n]
    zero_b = jnp.zeros((chunk, LANES), BF16)
    upd = [_dot_tn(jnp.concatenate([kw[j], bw[j]], axis=0),
                   jnp.concatenate([jnp.concatenate([vb[j], zero_b], axis=1),
                                    jnp.concatenate([-q_mat[j], p_mat[j]], axis=1)], axis=0)) for j in n]
    h_add = [jnp.where(same_head, upd[j][:, :LANES], 0.0) for j in n]
    bp = [jnp.where(same_head, upd[j][:, LANES:], 0.0).astype(BF16) for j in n]

    w_col = [jnp.sum(jnp.where(lane_eye, w_tot[i], 0.0), axis=1, keepdims=True) for i in n]
    state = [state_ref[j] for j in range(len(tiles))]
    ys = [None] * len(units)
    for u in (range(subs - 1, -1, -1) if reverse else range(subs)):
        for j in range(len(tiles)):
            i = u * len(tiles) + j
            h_b = state[j].astype(BF16)
            ys[i] = _dot(r_eff[i], h_b) + y0[i]
            state[j] = state[j] * w_col[i] - _dot(bp[i], h_b) + h_add[i]
    for j in range(len(tiles)):
        state_ref[j] = state[j]

    for i, (ru, sl) in enumerate(units):
        y = ys[i]
        d = y - head_sum(y) * (1.0 / HEAD)
        var = head_sum(d * d) * (1.0 / HEAD)
        out = d * lax.rsqrt(var + GN_EPS) * vec(_LN_W)[:, sl] + vec(_LN_B)[:, sl]
        out = out + head_sum(rl[ru, sl] * kt[ru, sl] * vec(_R_K)[:, sl]) * vl[ru, sl]
        if reverse:
            o_ref[ru, sl] = ((fwd_ref[ru, sl] + out) * _silu(z_ref[ru, sl])).astype(BF16)
        else:
            o_ref[ru, sl] = out


def _rwkv_dir(reverse, p_main, p_lora, vecs, mu_lora, w_up, a_up, fwd_out, batch, seq_all, ctx_len,
              col_r, col_k, col_v, col_z, gw):
    m = p_main.shape[0]
    chunk = RW_CHUNK
    assert 2 * chunk == LANES and chunk == HEAD
    rows = RW_BLOCK_CHUNKS * chunk
    assert ctx_len % rows == 0 and seq_all % rows == 0
    n_blocks = seq_all // rows
    n_ctx = ctx_len // rows
    per8 = rows // SUBLANES
    last8 = m // SUBLANES - 1
    nl = p_lora.shape[1]

    def block_of(i):
        if reverse:
            return jnp.where(i < n_ctx, n_ctx - 1 - i, n_blocks - 1 + n_ctx - i)
        return i

    def blk(b, i):
        return b * n_blocks + block_of(i)

    def halo(b, i):
        if reverse:
            return jnp.minimum((blk(b, i) + 1) * per8, last8)
        return jnp.maximum(blk(b, i) * per8 - 1, 0)

    tok = lambda col: pl.BlockSpec((rows, gw), lambda b, i: (blk(b, i), col))
    hal = lambda col: pl.BlockSpec((SUBLANES, gw), lambda b, i: (halo(b, i), col))
    const = lambda arr: pl.BlockSpec(arr.shape, lambda b, i: (0,) * arr.ndim)
    in_specs = [tok(col_r), tok(col_k), tok(col_v), pl.BlockSpec((rows, nl), lambda b, i: (blk(b, i), 0)),
                hal(col_r), hal(col_k), hal(col_v), pl.BlockSpec((SUBLANES, nl), lambda b, i: (halo(b, i), 0)),
                const(vecs), const(mu_lora), const(w_up), const(a_up)]
    args = [p_main, p_main, p_main, p_lora, p_main, p_main, p_main, p_lora, vecs, mu_lora, w_up, a_up]
    if reverse:
        in_specs += [pl.BlockSpec((rows, gw), lambda b, i: (blk(b, i), 0)), tok(col_z)]
        args += [fwd_out, p_main]
    return pl.pallas_call(
        functools.partial(_rwkv_kernel, reverse, chunk, n_ctx),
        grid=(batch, n_blocks),
        in_specs=in_specs,
        out_specs=pl.BlockSpec((rows, gw), lambda b, i: (blk(b, i), 0)),
        out_shape=jax.ShapeDtypeStruct((m, gw), BF16 if reverse else F32),
        scratch_shapes=[pltpu.VMEM((gw // LANES, LANES, LANES), F32)],
        compiler_params=_params("parallel", "arbitrary"),
        name="rwkv_rev" if reverse else "rwkv_fwd",
    )(*args)


def _rwkv(p_main, p_lora, rw, batch, seq_all, ctx_len, cols, gw):
    mu_rkv, mu_lora, w0, w_up, a0, a_up, k_k, k_a, r_k, ln_w, ln_b = rw
    out = None
    for d, reverse in enumerate((False, True)):
        rows = [mu_rkv[d, 0], mu_rkv[d, 1], mu_rkv[d, 2], w0[d], a0[d], k_k, k_a, r_k.reshape(-1), ln_w, ln_b]
        vecs = jnp.zeros((2 * SUBLANES, gw), F32).at[:len(rows)].set(jnp.stack(rows))
        zeros = jnp.zeros((RW_LORA, gw), F32)
        parts_w = [zeros] * 4
        parts_a = [zeros] * 4
        parts_w[d] = w_up[d]
        parts_a[2 + d] = a_up[d]
        mu = jnp.zeros((4, RW_LORA), F32).at[d].set(mu_lora[d, 0]).at[2 + d].set(mu_lora[d, 1])
        out = _rwkv_dir(reverse, p_main, p_lora, vecs, mu.reshape(1, 4 * RW_LORA),
                        jnp.concatenate(parts_w).astype(BF16), jnp.concatenate(parts_a).astype(BF16),
                        out, batch, seq_all, ctx_len, *cols, gw)
    return out


def _merge_kernel(tm, nb, first_pos, ctx_len, y0_ref, y1_ref, y2_ref, y3_ref, w_ref, h_ref, gb_ref, gc_ref, o_ref):
    gw = y0_ref.shape[1]
    acc = _dot(y0_ref[...], w_ref[0, 0:gw, :])
    for g, y_ref in enumerate((y1_ref, y2_ref, y3_ref), start=1):
        acc += _dot(y_ref[...], w_ref[0, g * gw:(g + 1) * gw, :])
    pos = first_pos + (pl.program_id(0) % nb) * tm + _iota((tm, 1), 0)
    gate = jnp.where(pos < ctx_len, gc_ref[0], gb_ref[0])
    o_ref[...] = h_ref[...] + gate * acc


def _merge(ys, w_out_all, layer, h, mod3, batch, seq_all, ctx_len, tn, latent_only):
    d = h.shape[1]
    gw = ys[0].shape[1]
    first_pos = ctx_len if latent_only else 0
    rows = seq_all - first_pos
    tm = _largest_tile(rows, 2 * LANES, 1024)
    nb = rows // tm

    def row_start(i):
        return pl.multiple_of((i // nb) * seq_all + first_pos + (i % nb) * tm, 2 * LANES)

    y_spec = pl.BlockSpec((pl.Element(tm), pl.Element(gw)), lambda i, j: (row_start(i), 0))
    gate_col = 2 * (d // tn)
    return pl.pallas_call(
        functools.partial(_merge_kernel, tm, nb, first_pos, ctx_len),
        grid=(batch * nb, d // tn),
        in_specs=[y_spec, y_spec, y_spec, y_spec,
                  pl.BlockSpec((1, w_out_all.shape[1], tn), lambda i, j: (layer, 0, j)),
                  pl.BlockSpec((pl.Element(tm), pl.Element(tn)), lambda i, j: (row_start(i), j * tn)),
                  pl.BlockSpec((1, 1, tn), lambda i, j: (i // nb, 0, gate_col + j)),
                  pl.BlockSpec((1, 1, tn), lambda i, j: (batch, 0, gate_col + j))],
        out_specs=pl.BlockSpec((tm, tn), lambda i, j: (i, j)),
        out_shape=jax.ShapeDtypeStruct((batch * rows, d), F32),
        compiler_params=_params("parallel", "parallel"),
        name="merge",
    )(*ys, w_out_all, h, mod3, mod3)


def kernel(x, c, ctx, c_ctx, w_mod, b_mod, w_in, w_out, pool_w, pool_scale, na_q_gain, na_k_gain, na_rpb, rw_mu_rkv, rw_mu_lora, rw_w0, rw_w_up, rw_a0, rw_a_up, rw_k_k, rw_k_a, rw_r_k, rw_ln_w, rw_ln_b, sg_ln_w, sg_ln_b, sg_w, sg_b):
    batch, seq, d = x.shape
    ctx_len = ctx.shape[1]
    seq_all = ctx_len + seq
    m = batch * seq_all
    depth = w_mod.shape[0]
    gw = d // 4
    assert batch < SUBLANES and gw % LANES == 0

    h = None
    cond = jnp.zeros((SUBLANES, d), F32).at[:batch].set(c).at[batch].set(c_ctx)
    mod = _modulation(cond, w_mod, b_mod)
    cos, sin = _rope_tables(seq, ctx_len)
    na_bias = _na_bias(na_rpb)
    tm = _largest_tile(seq_all, LANES, IN_PROJ_ROWS)
    tn = IN_PROJ_COLS

    lora_lo = 10 * gw
    lora_w = 4 * RW_LORA
    assert lora_lo % tn == 0 and gw % tn == 0
    w_out_b = w_out.astype(BF16)
    cols = {name: j for j, name in enumerate(("pool_v", "pool_z", "na_q", "na_k", "na_v", "na_z", "rw_r", "rw_k",
                                               "rw_v", "rw_z", "sg_u", "sg_v", "sg_z"))}

    for l in range(depth):
        mod3 = mod[l].reshape(SUBLANES, 1, 3 * d)
        if l == 0:
            xn, h = _norm_mod((ctx.reshape(batch * ctx_len, d), x.reshape(batch * seq, d)), mod3,
                              batch, seq_all, ctx_len)
        else:
            xn = _norm_mod((h,), mod3, batch, seq_all, ctx_len)
        p_main = _in_proj(xn, w_in, l, tm, tn, w_in.shape[2] - lora_w,
                          lambda j: jnp.where(j < lora_lo // tn, j * tn, j * tn + lora_w))
        p_lora = _in_proj(xn, w_in, l, tm, lora_w, lora_w, lambda j: lora_lo)

        y_pool, y_sg, qn, kn, vb = _local_mixers(
            p_main, cols, pool_w[l].astype(BF16), pool_scale[l], sg_ln_w[l], sg_ln_b[l], sg_w[l], sg_b[l],
            cos, sin, na_q_gain[l], na_k_gain[l], seq_all, ctx_len, gw)
        y_na = _na(qn, kn, vb, p_main, na_bias, l, batch, seq, ctx_len, cols["na_z"])
        rw = (rw_mu_rkv[l], rw_mu_lora[l], rw_w0[l], rw_w_up[l], rw_a0[l], rw_a_up[l],
              rw_k_k[l], rw_k_a[l], rw_r_k[l], rw_ln_w[l], rw_ln_b[l])
        y_rw = _rwkv(p_main, p_lora, rw, batch, seq_all, ctx_len, [cols[k] for k in ("rw_r", "rw_k", "rw_v", "rw_z")], gw)
        h = _merge((y_pool, y_na, y_rw, y_sg), w_out_b, l, h, mod3, batch, seq_all, ctx_len, gw,
                   latent_only=(l == depth - 1))

    return h.reshape(batch, seq, d)
```
